```python
import math
import jax, jax.numpy as jnp
from jax import lax
import numpy as np

D_MODEL = 1024
BATCH = 32
SEQ = 2048
DEPTH = 1

MEM_LEN = 256
D_MIX = D_MODEL
CONV_CH = D_MIX // 2
CONV_WIDTH = 31
DA_HEADS = 4
DA_HEAD_DIM = 64
DA_V_DIM = 2 * DA_HEAD_DIM
DA_WIDTH = DA_HEADS * DA_V_DIM
QK_COLS = DA_HEADS * 2 * DA_HEAD_DIM
IN_COLS = 2 * CONV_CH + 2 * QK_COLS + DA_WIDTH
X_HEADS = 4
X_HEAD_DIM = D_MODEL // X_HEADS
D_FF = 2816
ROPE_THETA = 10000.0
Q_BLOCK = 128
NORM_EPS = 1e-6
NEG_INF = -1e30

kernel_name = "hybrid_conformer_diffattn_macaron_block"


def rms_norm(x, g):
    xf = x.astype(jnp.float32)
    y = xf * lax.rsqrt(jnp.mean(xf * xf, axis=-1, keepdims=True) + NORM_EPS)
    return (y * g.astype(jnp.float32)).astype(x.dtype)


def layer_norm(x, g, b):
    xf = x.astype(jnp.float32)
    mu = jnp.mean(xf, axis=-1, keepdims=True)
    xc = xf - mu
    y = xc * lax.rsqrt(jnp.mean(xc * xc, axis=-1, keepdims=True) + NORM_EPS)
    return (y * g.astype(jnp.float32) + b.astype(jnp.float32)).astype(x.dtype)


def swiglu_ffn(x, w_gu, w_down):
    g, u = jnp.split(x @ w_gu, 2, axis=-1)
    return (jax.nn.silu(g) * u) @ w_down


def rope_tables(positions, dim, dtype):
    inv_freq = ROPE_THETA ** (-jnp.arange(0, dim, 2, dtype=jnp.float32) / dim)
    ang = positions.astype(jnp.float32)[..., None] * inv_freq
    return jnp.cos(ang)[:, :, None, :].astype(dtype), jnp.sin(ang)[:, :, None, :].astype(dtype)


def apply_rope(x, cos, sin):
    x1, x2 = jnp.split(x, 2, axis=-1)
    return jnp.concatenate([x1 * cos - x2 * sin, x2 * cos + x1 * sin], axis=-1)


def causal_depthwise_conv(u, w, b):
    k = w.shape[0]
    y = lax.conv_general_dilated(u, w[:, None, :].astype(u.dtype), window_strides=(1,),
                                 padding=[(k - 1, 0)],
                                 dimension_numbers=("NWC", "WIO", "NWC"),
                                 feature_group_count=u.shape[-1])
    return y + b


def differential_attention(q1, q2, k1, k2, v, lam):
    seq = q1.shape[1]
    scale = DA_HEAD_DIM ** -0.5
    outs = []
    for i in range(seq // Q_BLOCK):
        lo, hi = i * Q_BLOCK, (i + 1) * Q_BLOCK
        mask = jnp.arange(lo, hi)[:, None] >= jnp.arange(hi)[None, :]
        s1 = jnp.einsum("bqhd,bkhd->bhqk", q1[:, lo:hi], k1[:, :hi]).astype(jnp.float32) * scale
        s2 = jnp.einsum("bqhd,bkhd->bhqk", q2[:, lo:hi], k2[:, :hi]).astype(jnp.float32) * scale
        p1 = jax.nn.softmax(jnp.where(mask, s1, NEG_INF), axis=-1)
        p2 = jax.nn.softmax(jnp.where(mask, s2, NEG_INF), axis=-1)
        a = (p1 - lam * p2).astype(v.dtype)
        outs.append(jnp.einsum("bhqk,bkhd->bqhd", a, v[:, :hi]))
    return jnp.concatenate(outs, axis=1)


def hybrid_mixer(n, cos, sin, w_in, conv_w, conv_b, conv_ln_g, conv_ln_b,
                 lambda_q1, lambda_k1, lambda_q2, lambda_k2, da_norm_g, w_out, layer_idx):
    bsz, seq, _ = n.shape
    proj = n @ w_in
    conv_in, q, k, v = jnp.split(proj, [2 * CONV_CH, 2 * CONV_CH + QK_COLS,
                                        2 * CONV_CH + 2 * QK_COLS], axis=-1)
    a, g = jnp.split(conv_in, 2, axis=-1)
    u = a * jax.nn.sigmoid(g)
    u = causal_depthwise_conv(u, conv_w, conv_b)
    u = jax.nn.silu(layer_norm(u, conv_ln_g, conv_ln_b))
    q = apply_rope(q.reshape(bsz, seq, 2 * DA_HEADS, DA_HEAD_DIM), cos, sin)
    k = apply_rope(k.reshape(bsz, seq, 2 * DA_HEADS, DA_HEAD_DIM), cos, sin)
    q = q.reshape(bsz, seq, DA_HEADS, 2, DA_HEAD_DIM)
    k = k.reshape(bsz, seq, DA_HEADS, 2, DA_HEAD_DIM)
    v = v.reshape(bsz, seq, DA_HEADS, DA_V_DIM)
    lambda_init = 0.8 - 0.6 * math.exp(-0.3 * layer_idx)
    lam = (jnp.exp(jnp.sum(lambda_q1.astype(jnp.float32) * lambda_k1.astype(jnp.float32)))
           - jnp.exp(jnp.sum(lambda_q2.astype(jnp.float32) * lambda_k2.astype(jnp.float32)))
           + lambda_init)
    o = differential_attention(q[:, :, :, 0], q[:, :, :, 1], k[:, :, :, 0], k[:, :, :, 1], v, lam)
    o = rms_norm(o, da_norm_g) * (1.0 - lambda_init)
    o = o.reshape(bsz, seq, DA_WIDTH)
    return jnp.concatenate([u, o], axis=-1) @ w_out


def memory_cross_attention(h, mem_n, w_xq, w_xkv, w_xo):
    bsz, seq, _ = h.shape
    m = mem_n.shape[1]
    q = (h @ w_xq).reshape(bsz, seq, X_HEADS, X_HEAD_DIM)
    k, v = jnp.split(mem_n @ w_xkv, 2, axis=-1)
    k = k.reshape(bsz, m, X_HEADS, X_HEAD_DIM)
    v = v.reshape(bsz, m, X_HEADS, X_HEAD_DIM)
    s = jnp.einsum("bshd,bmhd->bhsm", q, k).astype(jnp.float32) * (X_HEAD_DIM ** -0.5)
    p = jax.nn.softmax(s, axis=-1).astype(v.dtype)
    o = jnp.einsum("bhsm,bmhd->bshd", p, v).reshape(bsz, seq, D_MODEL)
    return o @ w_xo


def setup_inputs(seed: int = 0) -> dict:
    key = jax.random.key(seed)
    ks = jax.random.split(key, 32)
    f32 = jnp.float32

    def w(k, shape, fan_in):
        return jax.random.normal(k, shape, f32) * (fan_in ** -0.5)

    def gain(k, shape):
        return 1.0 + 0.01 * jax.random.normal(k, shape, f32)

    def small(k, shape, s=0.01):
        return s * jax.random.normal(k, shape, f32)

    L = DEPTH
    x = jax.random.normal(ks[0], (BATCH, SEQ, D_MODEL), f32)
    mem = jax.random.normal(ks[1], (BATCH, MEM_LEN, D_MODEL), f32)
    offsets = jax.random.randint(ks[2], (BATCH, 1), 0, 1024, dtype=jnp.int32)
    positions = (offsets + jnp.arange(SEQ, dtype=jnp.int32)[None, :]).astype(jnp.int32)
    return {
        "x": x,
        "mem": mem,
        "positions": positions,
        "ffn1_norm": gain(ks[3], (L, D_MODEL)),
        "ffn1_w_gu": w(ks[4], (L, D_MODEL, 2 * D_FF), D_MODEL),
        "ffn1_w_down": w(ks[5], (L, D_FF, D_MODEL), D_FF),
        "mix_norm": gain(ks[6], (L, D_MODEL)),
        "w_in": w(ks[7], (L, D_MODEL, IN_COLS), D_MODEL),
        "conv_w": w(ks[8], (L, CONV_WIDTH, CONV_CH), CONV_WIDTH),
        "conv_b": small(ks[9], (L, CONV_CH)),
        "conv_ln_g": gain(ks[10], (L, CONV_CH)),
        "conv_ln_b": small(ks[11], (L, CONV_CH)),
        "lambda_q1": small(ks[12], (L, DA_HEAD_DIM), 0.1),
        "lambda_k1": small(ks[13], (L, DA_HEAD_DIM), 0.1),
        "lambda_q2": small(ks[14], (L, DA_HEAD_DIM), 0.1),
        "lambda_k2": small(ks[15], (L, DA_HEAD_DIM), 0.1),
        "da_norm_g": gain(ks[16], (L, DA_V_DIM)),
        "w_out": w(ks[17], (L, D_MIX, D_MODEL), D_MIX),
        "xattn_norm": gain(ks[18], (L, D_MODEL)),
        "mem_norm": gain(ks[19], (L, D_MODEL)),
        "w_xq": w(ks[20], (L, D_MODEL, D_MODEL), D_MODEL),
        "w_xkv": w(ks[21], (L, D_MODEL, 2 * D_MODEL), D_MODEL),
        "w_xo": w(ks[22], (L, D_MODEL, D_MODEL), D_MODEL),
        "ffn2_norm": gain(ks[23], (L, D_MODEL)),
        "ffn2_w_gu": w(ks[24], (L, D_MODEL, 2 * D_FF), D_MODEL),
        "ffn2_w_down": w(ks[25], (L, D_FF, D_MODEL), D_FF),
        "final_norm": gain(ks[26], (D_MODEL,)),
    }


def reference(x, mem, positions, ffn1_norm, ffn1_w_gu, ffn1_w_down, mix_norm, w_in,
              conv_w, conv_b, conv_ln_g, conv_ln_b, lambda_q1, lambda_k1, lambda_q2, lambda_k2,
              da_norm_g, w_out, xattn_norm, mem_norm, w_xq, w_xkv, w_xo,
              ffn2_norm, ffn2_w_gu, ffn2_w_down, final_norm):
    cos, sin = rope_tables(positions, DA_HEAD_DIM, x.dtype)
    h = x
    for l in range(DEPTH):
        h = h + 0.5 * swiglu_ffn(rms_norm(h, ffn1_norm[l]), ffn1_w_gu[l], ffn1_w_down[l])
        h = h + hybrid_mixer(rms_norm(h, mix_norm[l]), cos, sin, w_in[l], conv_w[l], conv_b[l],
                             conv_ln_g[l], conv_ln_b[l], lambda_q1[l], lambda_k1[l],
                             lambda_q2[l], lambda_k2[l], da_norm_g[l], w_out[l], l)
        h = h + memory_cross_attention(rms_norm(h, xattn_norm[l]), rms_norm(mem, mem_norm[l]),
                                       w_xq[l], w_xkv[l], w_xo[l])
        h = h + 0.5 * swiglu_ffn(rms_norm(h, ffn2_norm[l]), ffn2_w_gu[l], ffn2_w_down[l])
    return rms_norm(h, final_norm)
```

```python
import functools
import math

import jax
import jax.numpy as jnp
from jax import lax
from jax.experimental import pallas as pl
from jax.experimental.pallas import tpu as pltpu

F32 = jnp.float32
BF16 = jnp.bfloat16

NORM_EPS = 1e-6
NEG_INF = -1e30
ROPE_THETA = 10000.0
LOG2E = math.log2(math.e)

CONV_WIDTH = 31
DA_HEADS = 4
DA_HEAD_DIM = 64
X_HEADS = 4
LANES = 128
CONV_HALO = 32
VMEM_LIMIT = 56 * 1024 * 1024


def _rms(x, g):
    ms = jnp.mean(x * x, axis=-1, keepdims=True)
    return x * lax.rsqrt(ms + NORM_EPS) * g


def _sigmoid(x):
    return 1.0 / (1.0 + jnp.exp(-x))


def _dot(a, b):
    return jnp.dot(a, b, preferred_element_type=F32)


def _dot_nt(a, b):
    return lax.dot_general(a, b, (((1,), (1,)), ((), ())), preferred_element_type=F32)


def _resident(shape):
    return pl.BlockSpec(shape, lambda *_: (0,) * len(shape), pipeline_mode=pl.Buffered(1))


def _params(*sem):
    return pltpu.CompilerParams(dimension_semantics=sem, vmem_limit_bytes=VMEM_LIMIT)


def _tiles(n, t):
    if n % t:
        raise ValueError(f"extent {n} is not a multiple of tile {t}")
    return n // t


def _ffn_body(*refs, d_ff, chunks, final):
    if final:
        x_ref, g_ref, wgu_ref, wd_ref, fin_ref, o_ref, act_ref = refs
    else:
        x_ref, g_ref, wgu_ref, wd_ref, o_ref, act_ref = refs
    n = _rms(x_ref[...], g_ref[...]).astype(BF16)
    for c0, c1 in chunks:
        g = _dot(n, wgu_ref[:, c0:c1])
        u = _dot(n, wgu_ref[:, d_ff + c0:d_ff + c1])
        act_ref[:, c0:c1] = (g * _sigmoid(g) * u).astype(BF16)
    y = _dot(act_ref[...], wd_ref[...])
    h = x_ref[...] + 0.5 * y
    if final:
        h = _rms(h, fin_ref[...])
    o_ref[...] = h


def _ffn(x, norm_g, w_gu, w_down, final_g=None, *, tm=512):
    n_tok, d = x.shape
    d_ff = w_down.shape[0]
    step = 1024
    chunks = tuple((c, min(c + step, d_ff)) for c in range(0, d_ff, step))
    final = final_g is not None
    row = pl.BlockSpec((tm, d), lambda i: (i, 0))
    in_specs = [row, _resident((1, d)), _resident((d, 2 * d_ff)), _resident((d_ff, d))]
    args = [x, norm_g, w_gu, w_down]
    if final:
        in_specs.append(_resident((1, d)))
        args.append(final_g)
    return pl.pallas_call(
        functools.partial(_ffn_body, d_ff=d_ff, chunks=chunks, final=final),
        out_shape=jax.ShapeDtypeStruct((n_tok, d), F32),
        grid=(_tiles(n_tok, tm),),
        in_specs=in_specs,
        out_specs=row,
        scratch_shapes=[pltpu.VMEM((tm, d_ff), BF16)],
        compiler_params=_params("parallel"),
        name="ffn_final" if final else "ffn",
    )(*args)


def _inproj_body(h_ref, g_ref, w_ref, pos_ref, invf_ref, u_ref, q_ref, k_ref, v_ref,
                 *, conv_ch, qk_cols, q_scale):
    tm = h_ref.shape[0]
    n = _rms(h_ref[...], g_ref[...]).astype(BF16)
    ag = _dot(n, w_ref[:, 0:2 * conv_ch])
    u_ref[...] = ag[:, :conv_ch] * _sigmoid(ag[:, conv_ch:])
    ang = pos_ref[...] * invf_ref[...]
    lane = lax.broadcasted_iota(jnp.int32, (tm, LANES), 1)
    first_half = (lane % DA_HEAD_DIM) < (DA_HEAD_DIM // 2)
    cos = jnp.cos(ang)
    sin = jnp.sin(ang)
    sin = jnp.where(first_half, -sin, sin)
    half = DA_HEAD_DIM // 2

    def rope(x):
        partner = jnp.where(first_half, pltpu.roll(x, LANES - half, 1), pltpu.roll(x, half, 1))
        return x * cos + partner * sin

    q0 = 2 * conv_ch
    k0 = q0 + qk_cols
    v0 = k0 + qk_cols
    q = _dot(n, w_ref[:, q0:k0])
    k = _dot(n, w_ref[:, k0:v0])
    for c in range(0, qk_cols, LANES):
        q_ref[:, c:c + LANES] = (rope(q[:, c:c + LANES]) * q_scale).astype(BF16)
        k_ref[:, c:c + LANES] = rope(k[:, c:c + LANES]).astype(BF16)
    v_ref[...] = _dot(n, w_ref[:, v0:]).astype(BF16)


def _inproj(h, norm_g, w_in, pos, invf, *, conv_ch, qk_cols, tm=512):
    n_tok, d = h.shape
    in_cols = w_in.shape[1]
    v_cols = in_cols - 2 * conv_ch - 2 * qk_cols
    q_scale = DA_HEAD_DIM ** -0.5 * LOG2E
    row = lambda w: pl.BlockSpec((tm, w), lambda i: (i, 0))
    return pl.pallas_call(
        functools.partial(_inproj_body, conv_ch=conv_ch, qk_cols=qk_cols, q_scale=q_scale),
        out_shape=(jax.ShapeDtypeStruct((n_tok, conv_ch), F32),
                   jax.ShapeDtypeStruct((n_tok, qk_cols), BF16),
                   jax.ShapeDtypeStruct((n_tok, qk_cols), BF16),
                   jax.ShapeDtypeStruct((n_tok, v_cols), BF16)),
        grid=(_tiles(n_tok, tm),),
        in_specs=[row(d), _resident((1, d)), _resident((d, in_cols)), row(LANES),
                  _resident((1, LANES))],
        out_specs=(row(conv_ch), row(qk_cols), row(qk_cols), row(v_cols)),
        compiler_params=_params("parallel"),
        name="inproj",
    )(h, norm_g, w_in, pos, invf)


def _conv_body(u_ref, w_ref, b_ref, lg_ref, lb_ref, o_ref, win_ref, y_ref):
    ts, ch = u_ref.shape
    kw = w_ref.shape[0]

    @pl.when(pl.program_id(1) == 0)
    def _():
        win_ref[0:CONV_HALO, :] = jnp.zeros((CONV_HALO, ch), F32)

    @pl.when(pl.program_id(1) > 0)
    def _():
        win_ref[0:CONV_HALO, :] = win_ref[ts:ts + CONV_HALO, :]

    win_ref[CONV_HALO:CONV_HALO + ts, :] = u_ref[...]
    base = CONV_HALO - (kw - 1)
    for c in range(0, ch, LANES):
        acc = jnp.zeros((ts, LANES), F32)
        for j in range(kw):
            acc = acc + w_ref[j:j + 1, c:c + LANES] * win_ref[base + j:base + j + ts, c:c + LANES]
        y_ref[:, c:c + LANES] = acc
    y = y_ref[...] + b_ref[...]
    mu = jnp.mean(y, axis=-1, keepdims=True)
    yc = y - mu
    var = jnp.mean(yc * yc, axis=-1, keepdims=True)
    z = yc * lax.rsqrt(var + NORM_EPS) * lg_ref[...] + lb_ref[...]
    o_ref[...] = (z * _sigmoid(z)).astype(BF16)


def _conv(u, conv_w, conv_b, ln_g, ln_b, *, batch, ts=256):
    n_tok, ch = u.shape
    ns = _tiles(n_tok // batch, ts)
    kw = conv_w.shape[0]
    row = pl.BlockSpec((ts, ch), lambda b, s: (b * ns + s, 0))
    return pl.pallas_call(
        _conv_body,
        out_shape=jax.ShapeDtypeStruct((n_tok, ch), BF16),
        grid=(batch, ns),
        in_specs=[row, _resident((kw, ch)), _resident((1, ch)), _resident((1, ch)),
                  _resident((1, ch))],
        out_specs=row,
        scratch_shapes=[pltpu.VMEM((CONV_HALO + ts, ch), F32), pltpu.VMEM((ts, ch), F32)],
        compiler_params=_params("parallel", "arbitrary"),
        name="conv",
    )(u, conv_w, conv_b, ln_g, ln_b)


def _attn_body(lq1_ref, lk1_ref, lq2_ref, lk2_ref, dag_ref, q_ref, k_ref, v_ref, o_ref,
               m_ref, l_ref, acc_ref, *, lambda_init):
    tq = q_ref.shape[0]
    tk = tq
    i = pl.program_id(2)
    q = q_ref[...]
    lane = lax.broadcasted_iota(jnp.int32, (tq, LANES), 1)
    zero = jnp.zeros_like(q)
    qs = (jnp.where(lane < DA_HEAD_DIM, q, zero), jnp.where(lane >= DA_HEAD_DIM, q, zero))
    m_ref[...] = jnp.full(m_ref.shape, NEG_INF, F32)
    l_ref[...] = jnp.zeros(l_ref.shape, F32)
    acc_ref[...] = jnp.zeros(acc_ref.shape, F32)

    def block(j, masked):
        off = pl.multiple_of(j * tk, tk)
        k = k_ref[pl.ds(off, tk), :]
        v = v_ref[pl.ds(off, tk), :]
        for idx in range(2):
            s = _dot_nt(qs[idx], k)
            if masked:
                r = lax.broadcasted_iota(jnp.int32, (tq, tk), 0)
                c = lax.broadcasted_iota(jnp.int32, (tq, tk), 1)
                s = jnp.where(r >= c, s, NEG_INF)
            m_prev = m_ref[idx]
            m_new = jnp.maximum(m_prev, jnp.max(s, axis=-1, keepdims=True))
            alpha = jnp.exp2(m_prev - m_new)
            p = jnp.exp2(s - jnp.concatenate([m_new] * (tk // LANES), axis=-1))
            l_ref[idx] = alpha * l_ref[idx] + jnp.sum(p, axis=-1, keepdims=True)
            acc_ref[idx] = alpha * acc_ref[idx] + _dot(p.astype(BF16), v)
            m_ref[idx] = m_new

    def body(j, carry):
        block(j, False)
        return carry

    lax.fori_loop(0, i, body, 0)
    block(i, True)

    lam = (jnp.exp(jnp.sum(lq1_ref[...] * lk1_ref[...], axis=-1, keepdims=True))
           - jnp.exp(jnp.sum(lq2_ref[...] * lk2_ref[...], axis=-1, keepdims=True))
           + lambda_init)
    o = acc_ref[0] * (1.0 / l_ref[0]) - lam * (acc_ref[1] * (1.0 / l_ref[1]))
    o_ref[...] = (_rms(o, dag_ref[...]) * (1.0 - lambda_init)).astype(BF16)


def _attn(q, k, v, lq1, lk1, lq2, lk2, da_g, *, batch, seq, lambda_init, tq=256):
    n_tok, cols = q.shape
    heads = cols // LANES
    nq = _tiles(seq, tq)
    qspec = pl.BlockSpec((tq, LANES), lambda b, h, i: (b * nq + i, h))
    kvspec = pl.BlockSpec((seq, LANES), lambda b, h, i: (b, h))
    vec = lambda w: pl.BlockSpec((1, w), lambda b, h, i: (0, 0))
    return pl.pallas_call(
        functools.partial(_attn_body, lambda_init=lambda_init),
        out_shape=jax.ShapeDtypeStruct((n_tok, cols), BF16),
        grid=(batch, heads, nq),
        in_specs=[vec(DA_HEAD_DIM)] * 4 + [vec(LANES), qspec, kvspec, kvspec],
        out_specs=qspec,
        scratch_shapes=[pltpu.VMEM((2, tq, LANES), F32)] * 3,
        compiler_params=_params("parallel", "parallel", "arbitrary"),
        name="diff_attn",
    )(lq1, lk1, lq2, lk2, da_g, q, k, v)


def _memkv_body(m_ref, g_ref, w_ref, o_ref):
    n = _rms(m_ref[...], g_ref[...]).astype(BF16)
    o_ref[...] = _dot(n, w_ref[...]).astype(BF16)


def _memkv(mem, norm_g, w_xkv, *, tm=512):
    n_mem, d = mem.shape
    cols = w_xkv.shape[1]
    return pl.pallas_call(
        _memkv_body,
        out_shape=jax.ShapeDtypeStruct((n_mem, cols), BF16),
        grid=(_tiles(n_mem, tm),),
        in_specs=[pl.BlockSpec((tm, d), lambda i: (i, 0)), _resident((1, d)),
                  _resident((d, cols))],
        out_specs=pl.BlockSpec((tm, cols), lambda i: (i, 0)),
        compiler_params=_params("parallel"),
        name="memkv",
    )(mem, norm_g, w_xkv)


def _post_body(h_ref, c_ref, a_ref, wout_ref, xg_ref, wxq_ref, kv_ref, wxo_ref, o_ref, oc_ref,
               *, x_scale):
    d = h_ref.shape[1]
    conv_ch = c_ref.shape[1]
    hd = d // X_HEADS
    h2 = (h_ref[...] + _dot(c_ref[...], wout_ref[0:conv_ch, :])
          + _dot(a_ref[...], wout_ref[conv_ch:, :]))
    n = _rms(h2, xg_ref[...]).astype(BF16)
    q = (_dot(n, wxq_ref[...]) * x_scale).astype(BF16)
    for hh in range(X_HEADS):
        qh = q[:, hh * hd:(hh + 1) * hd]
        s = _dot_nt(qh, kv_ref[:, hh * hd:(hh + 1) * hd])
        p = jnp.exp2(s - jnp.max(s, axis=-1, keepdims=True))
        inv = 1.0 / jnp.sum(p, axis=-1, keepdims=True)
        o = _dot(p.astype(BF16), kv_ref[:, d + hh * hd:d + (hh + 1) * hd]) * inv
        oc_ref[:, hh * hd:(hh + 1) * hd] = o.astype(BF16)
    o_ref[...] = h2 + _dot(oc_ref[...], wxo_ref[...])


def _post(h, conv_o, attn_o, w_out, xg, w_xq, kv, w_xo, *, seq, mem_len, tm=512):
    n_tok, d = h.shape
    ch = conv_o.shape[1]
    per_b = _tiles(seq, tm)
    x_scale = (d // X_HEADS) ** -0.5 * LOG2E
    row = lambda w: pl.BlockSpec((tm, w), lambda i: (i, 0))
    return pl.pallas_call(
        functools.partial(_post_body, x_scale=x_scale),
        out_shape=jax.ShapeDtypeStruct((n_tok, d), F32),
        grid=(_tiles(n_tok, tm),),
        in_specs=[row(d), row(ch), row(attn_o.shape[1]), _resident(w_out.shape),
                  _resident((1, d)), _resident(w_xq.shape),
                  pl.BlockSpec((mem_len, 2 * d), lambda i: (i // per_b, 0)),
                  _resident(w_xo.shape)],
        out_specs=row(d),
        scratch_shapes=[pltpu.VMEM((tm, d), BF16)],
        compiler_params=_params("parallel"),
        name="post",
    )(h, conv_o, attn_o, w_out, xg, w_xq, kv, w_xo)


def kernel(x, mem, positions, ffn1_norm, ffn1_w_gu, ffn1_w_down, mix_norm, w_in, conv_w, conv_b,
           conv_ln_g, conv_ln_b, lambda_q1, lambda_k1, lambda_q2, lambda_k2, da_norm_g, w_out,
           xattn_norm, mem_norm, w_xq, w_xkv, w_xo, ffn2_norm, ffn2_w_gu, ffn2_w_down,
           final_norm):
    batch, seq, d = x.shape
    mem_len = mem.shape[1]
    depth = ffn1_norm.shape[0]
    conv_ch = conv_w.shape[2]
    qk_cols = DA_HEADS * 2 * DA_HEAD_DIM
    n_tok = batch * seq

    vec = lambda a: a.reshape(1, -1).astype(F32)
    bf = lambda a: a.astype(BF16)

    inv_freq = ROPE_THETA ** (-jnp.arange(0, DA_HEAD_DIM, 2, dtype=F32) / DA_HEAD_DIM)
    invf = jnp.tile(inv_freq, LANES // inv_freq.shape[0]).reshape(1, LANES)
    pos = jnp.broadcast_to(positions.astype(F32).reshape(n_tok, 1), (n_tok, LANES))

    h = x.reshape(n_tok, d)
    mem2 = mem.reshape(batch * mem_len, d)
    for l in range(depth):
        lambda_init = 0.8 - 0.6 * math.exp(-0.3 * l)
        h = _ffn(h, vec(ffn1_norm[l]), bf(ffn1_w_gu[l]), bf(ffn1_w_down[l]))
        u, q, k, v = _inproj(h, vec(mix_norm[l]), bf(w_in[l]), pos, invf,
                             conv_ch=conv_ch, qk_cols=qk_cols)
        conv_o = _conv(u, conv_w[l], vec(conv_b[l]), vec(conv_ln_g[l]), vec(conv_ln_b[l]),
                       batch=batch)
        attn_o = _attn(q, k, v, vec(lambda_q1[l]), vec(lambda_k1[l]), vec(lambda_q2[l]),
                       vec(lambda_k2[l]), vec(da_norm_g[l]), batch=batch, seq=seq,
                       lambda_init=lambda_init)
        kv = _memkv(mem2, vec(mem_norm[l]), bf(w_xkv[l]))
        h = _post(h, conv_o, attn_o, bf(w_out[l]), vec(xattn_norm[l]), bf(w_xq[l]), kv,
                  bf(w_xo[l]), seq=seq, mem_len=mem_len)
        last = l == depth - 1
        h = _ffn(h, vec(ffn2_norm[l]), bf(ffn2_w_gu[l]), bf(ffn2_w_down[l]),
                 vec(final_norm) if last else None)
    if depth == 0:
        raise ValueError("depth must be >= 1")
    return h.reshape(batch, seq, d)
```

```python
import functools
import math

import jax
import jax.numpy as jnp
from jax import lax
from jax.experimental import pallas as pl
from jax.experimental.pallas import tpu as pltpu

F32 = jnp.float32
BF16 = jnp.bfloat16

NORM_EPS = 1e-6
NEG_INF = -1e30
ROPE_THETA = 10000.0
LOG2E = math.log2(math.e)

CONV_WIDTH = 31
DA_HEADS = 4
DA_HEAD_DIM = 64
X_HEADS = 4
LANES = 128
CONV_HALO = 32
VMEM_LIMIT = 56 * 1024 * 1024


def _rms(x, g):
    ms = jnp.mean(x * x, axis=-1, keepdims=True)
    return x * lax.rsqrt(ms + NORM_EPS) * g


def _sigmoid(x):
    return 1.0 / (1.0 + jnp.exp(-x))


def _dot(a, b):
    return jnp.dot(a, b, preferred_element_type=F32)


def _dot_nt(a, b):
    return lax.dot_general(a, b, (((1,), (1,)), ((), ())), preferred_element_type=F32)


def _resident(shape):
    return pl.BlockSpec(shape, lambda *_: (0,) * len(shape), pipeline_mode=pl.Buffered(1))


def _params(*sem):
    return pltpu.CompilerParams(dimension_semantics=sem, vmem_limit_bytes=VMEM_LIMIT)


def _tiles(n, t):
    if n % t:
        raise ValueError(f"extent {n} is not a multiple of tile {t}")
    return n // t


def _ffn_body(*refs, d_ff, chunks, final):
    if final:
        x_ref, g_ref, wgu_ref, wd_ref, fin_ref, o_ref, act_ref = refs
    else:
        x_ref, g_ref, wgu_ref, wd_ref, o_ref, act_ref = refs
    n = _rms(x_ref[...], g_ref[...]).astype(BF16)
    for c0, c1 in chunks:
        g = _dot(n, wgu_ref[:, c0:c1])
        u = _dot(n, wgu_ref[:, d_ff + c0:d_ff + c1])
        act_ref[:, c0:c1] = (g * _sigmoid(g) * u).astype(BF16)
    y = _dot(act_ref[...], wd_ref[...])
    h = x_ref[...] + 0.5 * y
    if final:
        h = _rms(h, fin_ref[...])
    o_ref[...] = h


def _ffn(x, norm_g, w_gu, w_down, final_g=None, *, tm=512):
    n_tok, d = x.shape
    d_ff = w_down.shape[0]
    step = 1024
    chunks = tuple((c, min(c + step, d_ff)) for c in range(0, d_ff, step))
    final = final_g is not None
    row = pl.BlockSpec((tm, d), lambda i: (i, 0))
    in_specs = [row, _resident((1, d)), _resident((d, 2 * d_ff)), _resident((d_ff, d))]
    args = [x, norm_g, w_gu, w_down]
    if final:
        in_specs.append(_resident((1, d)))
        args.append(final_g)
    return pl.pallas_call(
        functools.partial(_ffn_body, d_ff=d_ff, chunks=chunks, final=final),
        out_shape=jax.ShapeDtypeStruct((n_tok, d), F32),
        grid=(_tiles(n_tok, tm),),
        in_specs=in_specs,
        out_specs=row,
        scratch_shapes=[pltpu.VMEM((tm, d_ff), BF16)],
        compiler_params=_params("parallel"),
        name="ffn_final" if final else "ffn",
    )(*args)


def _inproj_body(h_ref, g_ref, w_ref, pos_ref, invf_ref, u_ref, q_ref, k_ref, v_ref,
                 *, conv_ch, qk_cols, q_scale):
    tm = h_ref.shape[0]
    n = _rms(h_ref[...], g_ref[...]).astype(BF16)
    ag = _dot(n, w_ref[:, 0:2 * conv_ch])
    u_ref[...] = ag[:, :conv_ch] * _sigmoid(ag[:, conv_ch:])
    ang = pos_ref[...] * invf_ref[...]
    lane = lax.broadcasted_iota(jnp.int32, (tm, LANES), 1)
    first_half = (lane % DA_HEAD_DIM) < (DA_HEAD_DIM // 2)
    cos = jnp.cos(ang)
    sin = jnp.sin(ang)
    sin = jnp.where(first_half, -sin, sin)
    half = DA_HEAD_DIM // 2

    def rope(x):
        partner = jnp.where(first_half, pltpu.roll(x, LANES - half, 1), pltpu.roll(x, half, 1))
        return x * cos + partner * sin

    q0 = 2 * conv_ch
    k0 = q0 + qk_cols
    v0 = k0 + qk_cols
    q = _dot(n, w_ref[:, q0:k0])
    k = _dot(n, w_ref[:, k0:v0])
    for c in range(0, qk_cols, LANES):
        q_ref[:, c:c + LANES] = (rope(q[:, c:c + LANES]) * q_scale).astype(BF16)
        k_ref[:, c:c + LANES] = rope(k[:, c:c + LANES]).astype(BF16)
    v_ref[...] = _dot(n, w_ref[:, v0:]).astype(BF16)


def _inproj(h, norm_g, w_in, pos, invf, *, conv_ch, qk_cols, tm=512):
    n_tok, d = h.shape
    in_cols = w_in.shape[1]
    v_cols = in_cols - 2 * conv_ch - 2 * qk_cols
    q_scale = DA_HEAD_DIM ** -0.5 * LOG2E
    row = lambda w: pl.BlockSpec((tm, w), lambda i: (i, 0))
    return pl.pallas_call(
        functools.partial(_inproj_body, conv_ch=conv_ch, qk_cols=qk_cols, q_scale=q_scale),
        out_shape=(jax.ShapeDtypeStruct((n_tok, conv_ch), F32),
                   jax.ShapeDtypeStruct((n_tok, qk_cols), BF16),
                   jax.ShapeDtypeStruct((n_tok, qk_cols), BF16),
                   jax.ShapeDtypeStruct((n_tok, v_cols), BF16)),
        grid=(_tiles(n_tok, tm),),
        in_specs=[row(d), _resident((1, d)), _resident((d, in_cols)), row(LANES),
                  _resident((1, LANES))],
        out_specs=(row(conv_ch), row(qk_cols), row(qk_cols), row(v_cols)),
        compiler_params=_params("parallel"),
        name="inproj",
    )(h, norm_g, w_in, pos, invf)


def _conv_body(u_ref, w_ref, b_ref, lg_ref, lb_ref, o_ref, win_ref, y_ref, sh_ref):
    ts, ch = u_ref.shape
    kw = w_ref.shape[0]

    @pl.when(pl.program_id(1) == 0)
    def _():
        win_ref[0:CONV_HALO, :] = jnp.zeros((CONV_HALO, ch), F32)

    @pl.when(pl.program_id(1) > 0)
    def _():
        win_ref[0:CONV_HALO, :] = win_ref[ts:ts + CONV_HALO, :]

    win_ref[CONV_HALO:CONV_HALO + ts, :] = u_ref[...]
    base = CONV_HALO - (kw - 1)
    sub = 8
    chunk = 64
    for c in range(0, ch, LANES):
        for r in range(1, sub):
            rows = CONV_HALO + ts - sub
            sh_ref[r, 0:rows, :] = win_ref[r:r + rows, c:c + LANES]
        for t0 in range(0, ts, chunk):
            acc = jnp.zeros((chunk, LANES), F32)
            for j in range(kw):
                r = (base + j) % sub
                a = (base + j) // sub * sub + t0
                x = win_ref[a:a + chunk, c:c + LANES] if r == 0 else sh_ref[r, a:a + chunk, :]
                acc = acc + w_ref[j:j + 1, c:c + LANES] * x
            y_ref[t0:t0 + chunk, c:c + LANES] = acc
    y = y_ref[...] + b_ref[...]
    mu = jnp.mean(y, axis=-1, keepdims=True)
    yc = y - mu
    var = jnp.mean(yc * yc, axis=-1, keepdims=True)
    z = yc * lax.rsqrt(var + NORM_EPS) * lg_ref[...] + lb_ref[...]
    o_ref[...] = (z * _sigmoid(z)).astype(BF16)


def _conv(u, conv_w, conv_b, ln_g, ln_b, *, batch, ts=256):
    n_tok, ch = u.shape
    ns = _tiles(n_tok // batch, ts)
    kw = conv_w.shape[0]
    row = pl.BlockSpec((ts, ch), lambda b, s: (b * ns + s, 0))
    return pl.pallas_call(
        _conv_body,
        out_shape=jax.ShapeDtypeStruct((n_tok, ch), BF16),
        grid=(batch, ns),
        in_specs=[row, _resident((kw, ch)), _resident((1, ch)), _resident((1, ch)),
                  _resident((1, ch))],
        out_specs=row,
        scratch_shapes=[pltpu.VMEM((CONV_HALO + ts, ch), F32), pltpu.VMEM((ts, ch), F32),
                        pltpu.VMEM((8, CONV_HALO + ts, LANES), F32)],
        compiler_params=_params("parallel", "arbitrary"),
        name="conv",
    )(u, conv_w, conv_b, ln_g, ln_b)


def _attn_body(lq1_ref, lk1_ref, lq2_ref, lk2_ref, dag_ref, q_ref, k_ref, v_ref, o_ref,
               *, lambda_init, tq):
    seq = q_ref.shape[0]
    tk = tq
    nq = seq // tq
    lane = lax.broadcasted_iota(jnp.int32, (tq, LANES), 1)
    key_le_query = (lax.broadcasted_iota(jnp.int32, (tk, tq), 0)
                    <= lax.broadcasted_iota(jnp.int32, (tk, tq), 1))
    lam = (jnp.exp(jnp.sum(lq1_ref[...] * lk1_ref[...], axis=-1, keepdims=True))
           - jnp.exp(jnp.sum(lq2_ref[...] * lk2_ref[...], axis=-1, keepdims=True))
           + lambda_init)
    vt = v_ref[...].astype(F32).T.astype(BF16)

    def scores(i):
        q = q_ref[i * tq:(i + 1) * tq, :]
        zero = jnp.zeros_like(q)
        qs = (jnp.where(lane < DA_HEAD_DIM, q, zero), jnp.where(lane >= DA_HEAD_DIM, q, zero))
        keys = k_ref[0:(i + 1) * tk, :]
        return [_dot_nt(keys, qz) for qz in qs]

    def probs(st, scale):
        n_full = st.shape[0] - tk
        diag = jnp.where(key_le_query, st[n_full:], NEG_INF)
        m = jnp.max(diag, axis=0, keepdims=True)
        if n_full:
            full = st[:n_full]
            m = jnp.maximum(m, jnp.max(full, axis=0, keepdims=True))
            parts = [jnp.exp2(full - m), jnp.exp2(diag - m)]
        else:
            parts = [jnp.exp2(diag - m)]
        l = sum(jnp.sum(p, axis=0, keepdims=True) for p in parts)
        w = scale * (1.0 / l)
        return [p * w for p in parts]

    st_next = scores(0)
    for i in range(nq):
        st = st_next
        if i + 1 < nq:
            st_next = scores(i + 1)
        p1 = probs(st[0], 1.0)
        p2 = probs(st[1], lam)
        a = jnp.concatenate([(x - y).astype(BF16) for x, y in zip(p1, p2)], axis=0)
        o = _dot(vt[:, 0:(i + 1) * tk], a).T
        o_ref[i * tq:(i + 1) * tq, :] = (_rms(o, dag_ref[...]) * (1.0 - lambda_init)).astype(BF16)


def _attn(q, k, v, lq1, lk1, lq2, lk2, da_g, *, batch, seq, lambda_init, tq=256):
    n_tok, cols = q.shape
    heads = cols // LANES
    _tiles(seq, tq)
    blk = pl.BlockSpec((seq, LANES), lambda b, h: (b, h))
    vec = lambda w: pl.BlockSpec((1, w), lambda b, h: (0, 0))
    return pl.pallas_call(
        functools.partial(_attn_body, lambda_init=lambda_init, tq=tq),
        out_shape=jax.ShapeDtypeStruct((n_tok, cols), BF16),
        grid=(batch, heads),
        in_specs=[vec(DA_HEAD_DIM)] * 4 + [vec(LANES), blk, blk, blk],
        out_specs=blk,
        compiler_params=_params("parallel", "parallel"),
        name="diff_attn",
    )(lq1, lk1, lq2, lk2, da_g, q, k, v)


def _memkv_body(m_ref, g_ref, w_ref, o_ref):
    n = _rms(m_ref[...], g_ref[...]).astype(BF16)
    o_ref[...] = _dot(n, w_ref[...]).astype(BF16)


def _memkv(mem, norm_g, w_xkv, *, tm=512):
    n_mem, d = mem.shape
    cols = w_xkv.shape[1]
    return pl.pallas_call(
        _memkv_body,
        out_shape=jax.ShapeDtypeStruct((n_mem, cols), BF16),
        grid=(_tiles(n_mem, tm),),
        in_specs=[pl.BlockSpec((tm, d), lambda i: (i, 0)), _resident((1, d)),
                  _resident((d, cols))],
        out_specs=pl.BlockSpec((tm, cols), lambda i: (i, 0)),
        compiler_params=_params("parallel"),
        name="memkv",
    )(mem, norm_g, w_xkv)


def _post_body(h_ref, c_ref, a_ref, wout_ref, xg_ref, wxq_ref, kv_ref, wxo_ref, o_ref, oc_ref,
               *, x_scale):
    d = h_ref.shape[1]
    conv_ch = c_ref.shape[1]
    hd = d // X_HEADS
    h2 = (h_ref[...] + _dot(c_ref[...], wout_ref[0:conv_ch, :])
          + _dot(a_ref[...], wout_ref[conv_ch:, :]))
    n = _rms(h2, xg_ref[...]).astype(BF16)
    q = (_dot(n, wxq_ref[...]) * x_scale).astype(BF16)
    for hh in range(X_HEADS):
        qh = q[:, hh * hd:(hh + 1) * hd]
        s = _dot_nt(qh, kv_ref[:, hh * hd:(hh + 1) * hd])
        p = jnp.exp2(s - jnp.max(s, axis=-1, keepdims=True))
        inv = 1.0 / jnp.sum(p, axis=-1, keepdims=True)
        o = _dot(p.astype(BF16), kv_ref[:, d + hh * hd:d + (hh + 1) * hd]) * inv
        oc_ref[:, hh * hd:(hh + 1) * hd] = o.astype(BF16)
    o_ref[...] = h2 + _dot(oc_ref[...], wxo_ref[...])


def _post(h, conv_o, attn_o, w_out, xg, w_xq, kv, w_xo, *, seq, mem_len, tm=512):
    n_tok, d = h.shape
    ch = conv_o.shape[1]
    per_b = _tiles(seq, tm)
    x_scale = (d // X_HEADS) ** -0.5 * LOG2E
    row = lambda w: pl.BlockSpec((tm, w), lambda i: (i, 0))
    return pl.pallas_call(
        functools.partial(_post_body, x_scale=x_scale),
        out_shape=jax.ShapeDtypeStruct((n_tok, d), F32),
        grid=(_tiles(n_tok, tm),),
        in_specs=[row(d), row(ch), row(attn_o.shape[1]), _resident(w_out.shape),
                  _resident((1, d)), _resident(w_xq.shape),
                  pl.BlockSpec((mem_len, 2 * d), lambda i: (i // per_b, 0)),
                  _resident(w_xo.shape)],
        out_specs=row(d),
        scratch_shapes=[pltpu.VMEM((tm, d), BF16)],
        compiler_params=_params("parallel"),
        name="post",
    )(h, conv_o, attn_o, w_out, xg, w_xq, kv, w_xo)


def kernel(x, mem, positions, ffn1_norm, ffn1_w_gu, ffn1_w_down, mix_norm, w_in, conv_w, conv_b,
           conv_ln_g, conv_ln_b, lambda_q1, lambda_k1, lambda_q2, lambda_k2, da_norm_g, w_out,
           xattn_norm, mem_norm, w_xq, w_xkv, w_xo, ffn2_norm, ffn2_w_gu, ffn2_w_down,
           final_norm):
    batch, seq, d = x.shape
    mem_len = mem.shape[1]
    depth = ffn1_norm.shape[0]
    conv_ch = conv_w.shape[2]
    qk_cols = DA_HEADS * 2 * DA_HEAD_DIM
    n_tok = batch * seq

    vec = lambda a: a.reshape(1, -1).astype(F32)
    bf = lambda a: a.astype(BF16)

    inv_freq = ROPE_THETA ** (-jnp.arange(0, DA_HEAD_DIM, 2, dtype=F32) / DA_HEAD_DIM)
    invf = jnp.tile(inv_freq, LANES // inv_freq.shape[0]).reshape(1, LANES)
    pos = jnp.broadcast_to(positions.astype(F32).reshape(n_tok, 1), (n_tok, LANES))

    h = x.reshape(n_tok, d)
    mem2 = mem.reshape(batch * mem_len, d)
    for l in range(depth):
        lambda_init = 0.8 - 0.6 * math.exp(-0.3 * l)
        h = _ffn(h, vec(ffn1_norm[l]), bf(ffn1_w_gu[l]), bf(ffn1_w_down[l]))
        u, q, k, v = _inproj(h, vec(mix_norm[l]), bf(w_in[l]), pos, invf,
                             conv_ch=conv_ch, qk_cols=qk_cols)
        conv_o = _conv(u, conv_w[l], vec(conv_b[l]), vec(conv_ln_g[l]), vec(conv_ln_b[l]),
                       batch=batch)
        attn_o = _attn(q, k, v, vec(lambda_q1[l]), vec(lambda_k1[l]), vec(lambda_q2[l]),
                       vec(lambda_k2[l]), vec(da_norm_g[l]), batch=batch, seq=seq,
                       lambda_init=lambda_init)
        kv = _memkv(mem2, vec(mem_norm[l]), bf(w_xkv[l]))
        h = _post(h, conv_o, attn_o, bf(w_out[l]), vec(xattn_norm[l]), bf(w_xq[l]), kv,
                  bf(w_xo[l]), seq=seq, mem_len=mem_len)
        last = l == depth - 1
        h = _ffn(h, vec(ffn2_norm[l]), bf(ffn2_w_gu[l]), bf(ffn2_w_down[l]),
                 vec(final_norm) if last else None)
    if depth == 0:
        raise ValueError("depth must be >= 1")
    return h.reshape(batch, seq, d)
```

```python
import functools
import math

import jax
import jax.numpy as jnp
from jax import lax
from jax.experimental import pallas as pl
from jax.experimental.pallas import tpu as pltpu

F32 = jnp.float32
BF16 = jnp.bfloat16

NORM_EPS = 1e-6
NEG_INF = -1e30
ROPE_THETA = 10000.0
LOG2E = math.log2(math.e)

DA_HEADS = 4
DA_HEAD_DIM = 64
X_HEADS = 4
LANES = 128
SUBLANES = 8
CONV_HALO = 32
VMEM_LIMIT = 56 * 1024 * 1024


def _rms(x, g):
    ms = jnp.mean(x * x, axis=-1, keepdims=True)
    return x * lax.rsqrt(ms + NORM_EPS) * g


def _sigmoid(x):
    return 1.0 / (1.0 + jnp.exp(-x))


def _dot(a, b):
    return jnp.dot(a, b, preferred_element_type=F32)


def _dot_nt(a, b):
    return lax.dot_general(a, b, (((1,), (1,)), ((), ())), preferred_element_type=F32)


def _resident(shape):
    return pl.BlockSpec(shape, lambda *_: (0,) * len(shape), pipeline_mode=pl.Buffered(1))


def _params(*sem):
    return pltpu.CompilerParams(dimension_semantics=sem, vmem_limit_bytes=VMEM_LIMIT)


def _tiles(n, t):
    if n % t:
        raise ValueError(f"extent {n} is not a multiple of tile {t}")
    return n // t


def _ffn_tile(x, g_ref, wgu_ref, wd_ref, act_ref, chunks):
    d_ff = wd_ref.shape[0]
    n = _rms(x, g_ref[...]).astype(BF16)
    for c0, c1 in chunks:
        g = _dot(n, wgu_ref[:, c0:c1])
        u = _dot(n, wgu_ref[:, d_ff + c0:d_ff + c1])
        act_ref[:, c0:c1] = (g * _sigmoid(g) * u).astype(BF16)
    return x + 0.5 * _dot(act_ref[...], wd_ref[...])


def _ffn_chunks(d_ff, step=1024):
    return tuple((c, min(c + step, d_ff)) for c in range(0, d_ff, step))


def _inproj_tile(h, g_ref, w_ref, pos_ref, invf_ref, u_ref, q_ref, k_ref, v_ref, *, q_scale):
    tm = h.shape[0]
    conv_ch = u_ref.shape[1]
    qk_cols = q_ref.shape[1]
    n = _rms(h, g_ref[...]).astype(BF16)
    ag = _dot(n, w_ref[:, 0:2 * conv_ch])
    u_ref[...] = ag[:, :conv_ch] * _sigmoid(ag[:, conv_ch:])
    ang = pos_ref[...] * invf_ref[...]
    lane = lax.broadcasted_iota(jnp.int32, (tm, LANES), 1)
    half = DA_HEAD_DIM // 2
    first_half = (lane % DA_HEAD_DIM) < half
    cos = jnp.cos(ang)
    sin = jnp.sin(ang)
    sin = jnp.where(first_half, -sin, sin)

    def rope(x):
        partner = jnp.where(first_half, pltpu.roll(x, LANES - half, 1), pltpu.roll(x, half, 1))
        return x * cos + partner * sin

    q0 = 2 * conv_ch
    k0 = q0 + qk_cols
    v0 = k0 + qk_cols
    q = _dot(n, w_ref[:, q0:k0])
    k = _dot(n, w_ref[:, k0:v0])
    for c in range(0, qk_cols, LANES):
        q_ref[:, c:c + LANES] = (rope(q[:, c:c + LANES]) * q_scale).astype(BF16)
        k_ref[:, c:c + LANES] = rope(k[:, c:c + LANES]).astype(BF16)
    v_ref[...] = _dot(n, w_ref[:, v0:]).astype(BF16)


def _conv_tile(win_ref, sh_ref, y_ref, w_ref, b_ref, lg_ref, lb_ref):
    ts, ch = y_ref.shape
    kw = w_ref.shape[0]
    base = CONV_HALO - (kw - 1)
    chunk = 32
    for c in range(0, ch, LANES):
        for r in range(1, SUBLANES):
            rows = CONV_HALO + ts - SUBLANES
            sh_ref[r, 0:rows, :] = win_ref[r:r + rows, c:c + LANES]
        for t0 in range(0, ts, chunk):
            acc = jnp.zeros((chunk, LANES), F32)
            for j in range(kw):
                r = (base + j) % SUBLANES
                a = (base + j) // SUBLANES * SUBLANES + t0
                x = win_ref[a:a + chunk, c:c + LANES] if r == 0 else sh_ref[r, a:a + chunk, :]
                acc = acc + w_ref[j:j + 1, c:c + LANES] * x
            y_ref[t0:t0 + chunk, c:c + LANES] = acc
    y = y_ref[...] + b_ref[...]
    mu = jnp.mean(y, axis=-1, keepdims=True)
    yc = y - mu
    var = jnp.mean(yc * yc, axis=-1, keepdims=True)
    z = yc * lax.rsqrt(var + NORM_EPS) * lg_ref[...] + lb_ref[...]
    return (z * _sigmoid(z)).astype(BF16)


def _front_body(x_ref, pos_ref, g1_ref, wgu_ref, wd_ref, gm_ref, win_w_ref, invf_ref,
                cw_ref, cb_ref, lg_ref, lb_ref,
                h_ref, q_ref, k_ref, v_ref, c_ref,
                act_ref, u_ref, win_ref, sh_ref, y_ref, *, chunks, q_scale, tiles_per_seq):
    i = pl.program_id(0)
    tm = x_ref.shape[0]

    @pl.when(i == 0)
    def _():
        win_ref[...] = jnp.zeros(win_ref.shape, F32)
        u_ref[...] = jnp.zeros(u_ref.shape, F32)

    seq_start = (i - 1) % tiles_per_seq == 0
    halo = win_ref[tm:tm + CONV_HALO, :]
    win_ref[0:CONV_HALO, :] = jnp.where(seq_start, jnp.zeros_like(halo), halo)
    win_ref[CONV_HALO:CONV_HALO + tm, :] = u_ref[...]
    c_ref[...] = _conv_tile(win_ref, sh_ref, y_ref, cw_ref, cb_ref, lg_ref, lb_ref)

    h = _ffn_tile(x_ref[...], g1_ref, wgu_ref, wd_ref, act_ref, chunks)
    h_ref[...] = h
    _inproj_tile(h, gm_ref, win_w_ref, pos_ref, invf_ref, u_ref, q_ref, k_ref, v_ref,
                 q_scale=q_scale)


def _front(x, pos, g1, w_gu, w_down, gm, w_in, invf, conv_w, conv_b, ln_g, ln_b,
           *, seq, qk_cols, tm=512):
    n_tok, d = x.shape
    d_ff = w_down.shape[0]
    kw, conv_ch = conv_w.shape
    in_cols = w_in.shape[1]
    v_cols = in_cols - 2 * conv_ch - 2 * qk_cols
    n = _tiles(n_tok, tm)
    cur = lambda w: pl.BlockSpec((tm, w), lambda i: (jnp.minimum(i, n - 1), 0))
    lag = lambda w: pl.BlockSpec((tm, w), lambda i: (jnp.maximum(i - 1, 0), 0))
    return pl.pallas_call(
        functools.partial(_front_body, chunks=_ffn_chunks(d_ff),
                          q_scale=DA_HEAD_DIM ** -0.5 * LOG2E, tiles_per_seq=_tiles(seq, tm)),
        out_shape=(jax.ShapeDtypeStruct((n_tok, d), F32),
                   jax.ShapeDtypeStruct((n_tok, qk_cols), BF16),
                   jax.ShapeDtypeStruct((n_tok, qk_cols), BF16),
                   jax.ShapeDtypeStruct((n_tok, v_cols), BF16),
                   jax.ShapeDtypeStruct((n_tok, conv_ch), BF16)),
        grid=(n + 1,),
        in_specs=[cur(d), cur(LANES), _resident((1, d)), _resident((d, 2 * d_ff)),
                  _resident((d_ff, d)), _resident((1, d)), _resident((d, in_cols)),
                  _resident((1, LANES)), _resident((kw, conv_ch)), _resident((1, conv_ch)),
                  _resident((1, conv_ch)), _resident((1, conv_ch))],
        out_specs=(cur(d), cur(qk_cols), cur(qk_cols), cur(v_cols), lag(conv_ch)),
        scratch_shapes=[pltpu.VMEM((tm, d_ff), BF16),
                        pltpu.VMEM((tm, conv_ch), F32),
                        pltpu.VMEM((CONV_HALO + tm, conv_ch), F32),
                        pltpu.VMEM((SUBLANES, CONV_HALO + tm, LANES), F32),
                        pltpu.VMEM((tm, conv_ch), F32)],
        compiler_params=_params("arbitrary"),
        name="front",
    )(x, pos, g1, w_gu, w_down, gm, w_in, invf, conv_w, conv_b, ln_g, ln_b)


def _attn_body(lq1_ref, lk1_ref, lq2_ref, lk2_ref, dag_ref, q_ref, k_ref, v_ref, o_ref,
               *, lambda_init, tq):
    seq = q_ref.shape[0]
    tk = tq
    nq = seq // tq
    lane = lax.broadcasted_iota(jnp.int32, (tq, LANES), 1)
    key_le_query = (lax.broadcasted_iota(jnp.int32, (tk, tq), 0)
                    <= lax.broadcasted_iota(jnp.int32, (tk, tq), 1))
    lam = (jnp.exp(jnp.sum(lq1_ref[...] * lk1_ref[...], axis=-1, keepdims=True))
           - jnp.exp(jnp.sum(lq2_ref[...] * lk2_ref[...], axis=-1, keepdims=True))
           + lambda_init)
    vt = v_ref[...].astype(F32).T.astype(BF16)

    def scores(i):
        q = q_ref[i * tq:(i + 1) * tq, :]
        zero = jnp.zeros_like(q)
        qs = (jnp.where(lane < DA_HEAD_DIM, q, zero), jnp.where(lane >= DA_HEAD_DIM, q, zero))
        keys = k_ref[0:(i + 1) * tk, :]
        return [_dot_nt(keys, qz) for qz in qs]

    def probs(st, scale):
        n_full = st.shape[0] - tk
        diag = jnp.where(key_le_query, st[n_full:], NEG_INF)
        m = jnp.max(diag, axis=0, keepdims=True)
        if n_full:
            full = st[:n_full]
            m = jnp.maximum(m, jnp.max(full, axis=0, keepdims=True))
            parts = [jnp.exp2(full - m), jnp.exp2(diag - m)]
        else:
            parts = [jnp.exp2(diag - m)]
        l = sum(jnp.sum(p, axis=0, keepdims=True) for p in parts)
        w = scale * (1.0 / l)
        return [p * w for p in parts]

    st_next = scores(0)
    for i in range(nq):
        st = st_next
        if i + 1 < nq:
            st_next = scores(i + 1)
        p1 = probs(st[0], 1.0)
        p2 = probs(st[1], lam)
        a = jnp.concatenate([(x - y).astype(BF16) for x, y in zip(p1, p2)], axis=0)
        o = _dot(vt[:, 0:(i + 1) * tk], a).T
        o_ref[i * tq:(i + 1) * tq, :] = (_rms(o, dag_ref[...]) * (1.0 - lambda_init)).astype(BF16)


def _attn(q, k, v, lq1, lk1, lq2, lk2, da_g, *, batch, seq, lambda_init, tq=256):
    n_tok, cols = q.shape
    heads = cols // LANES
    _tiles(seq, tq)
    blk = pl.BlockSpec((seq, LANES), lambda b, h: (b, h))
    vec = lambda w: pl.BlockSpec((1, w), lambda b, h: (0, 0))
    return pl.pallas_call(
        functools.partial(_attn_body, lambda_init=lambda_init, tq=tq),
        out_shape=jax.ShapeDtypeStruct((n_tok, cols), BF16),
        grid=(batch, heads),
        in_specs=[vec(DA_HEAD_DIM)] * 4 + [vec(LANES), blk, blk, blk],
        out_specs=blk,
        compiler_params=_params("parallel", "parallel"),
        name="diff_attn",
    )(lq1, lk1, lq2, lk2, da_g, q, k, v)


def _memkv_body(m_ref, g_ref, w_ref, o_ref):
    n = _rms(m_ref[...], g_ref[...]).astype(BF16)
    o_ref[...] = _dot(n, w_ref[...]).astype(BF16)


def _memkv(mem, norm_g, w_xkv, *, tm=512):
    n_mem, d = mem.shape
    cols = w_xkv.shape[1]
    return pl.pallas_call(
        _memkv_body,
        out_shape=jax.ShapeDtypeStruct((n_mem, cols), BF16),
        grid=(_tiles(n_mem, tm),),
        in_specs=[pl.BlockSpec((tm, d), lambda i: (i, 0)), _resident((1, d)),
                  _resident((d, cols))],
        out_specs=pl.BlockSpec((tm, cols), lambda i: (i, 0)),
        compiler_params=_params("parallel"),
        name="memkv",
    )(mem, norm_g, w_xkv)


def _back_body(*refs, x_scale, chunks, final):
    (h_ref, c_ref, a_ref, wout_ref, xg_ref, wxq_ref, kv_ref, wxo_ref,
     g2_ref, wgu_ref, wd_ref) = refs[:11]
    fin_ref = refs[11] if final else None
    o_ref, oc_ref, act_ref = refs[-3:]
    d = h_ref.shape[1]
    conv_ch = c_ref.shape[1]
    hd = d // X_HEADS
    h2 = (h_ref[...] + _dot(c_ref[...], wout_ref[0:conv_ch, :])
          + _dot(a_ref[...], wout_ref[conv_ch:, :]))
    n = _rms(h2, xg_ref[...]).astype(BF16)
    q = (_dot(n, wxq_ref[...]) * x_scale).astype(BF16)
    for hh in range(X_HEADS):
        qh = q[:, hh * hd:(hh + 1) * hd]
        s = _dot_nt(qh, kv_ref[:, hh * hd:(hh + 1) * hd])
        p = jnp.exp2(s - jnp.max(s, axis=-1, keepdims=True))
        inv = 1.0 / jnp.sum(p, axis=-1, keepdims=True)
        o = _dot(p.astype(BF16), kv_ref[:, d + hh * hd:d + (hh + 1) * hd]) * inv
        oc_ref[:, hh * hd:(hh + 1) * hd] = o.astype(BF16)
    h3 = h2 + _dot(oc_ref[...], wxo_ref[...])
    y = _ffn_tile(h3, g2_ref, wgu_ref, wd_ref, act_ref, chunks)
    if final:
        y = _rms(y, fin_ref[...])
    o_ref[...] = y


def _back(h, conv_o, attn_o, w_out, xg, w_xq, kv, w_xo, g2, w_gu, w_down, final_g=None,
          *, seq, mem_len, tm=512):
    n_tok, d = h.shape
    ch = conv_o.shape[1]
    d_ff = w_down.shape[0]
    per_b = _tiles(seq, tm)
    x_scale = (d // X_HEADS) ** -0.5 * LOG2E
    final = final_g is not None
    row = lambda w: pl.BlockSpec((tm, w), lambda i: (i, 0))
    in_specs = [row(d), row(ch), row(attn_o.shape[1]), _resident(w_out.shape),
                _resident((1, d)), _resident(w_xq.shape),
                pl.BlockSpec((mem_len, 2 * d), lambda i: (i // per_b, 0)),
                _resident(w_xo.shape), _resident((1, d)), _resident((d, 2 * d_ff)),
                _resident((d_ff, d))]
    args = [h, conv_o, attn_o, w_out, xg, w_xq, kv, w_xo, g2, w_gu, w_down]
    if final:
        in_specs.append(_resident((1, d)))
        args.append(final_g)
    return pl.pallas_call(
        functools.partial(_back_body, x_scale=x_scale, chunks=_ffn_chunks(d_ff), final=final),
        out_shape=jax.ShapeDtypeStruct((n_tok, d), F32),
        grid=(_tiles(n_tok, tm),),
        in_specs=in_specs,
        out_specs=row(d),
        scratch_shapes=[pltpu.VMEM((tm, d), BF16), pltpu.VMEM((tm, d_ff), BF16)],
        compiler_params=_params("parallel"),
        name="back",
    )(*args)


def kernel(x, mem, positions, ffn1_norm, ffn1_w_gu, ffn1_w_down, mix_norm, w_in, conv_w, conv_b,
           conv_ln_g, conv_ln_b, lambda_q1, lambda_k1, lambda_q2, lambda_k2, da_norm_g, w_out,
           xattn_norm, mem_norm, w_xq, w_xkv, w_xo, ffn2_norm, ffn2_w_gu, ffn2_w_down,
           final_norm):
    batch, seq, d = x.shape
    mem_len = mem.shape[1]
    depth = ffn1_norm.shape[0]
    qk_cols = DA_HEADS * 2 * DA_HEAD_DIM
    n_tok = batch * seq

    vec = lambda a: a.reshape(1, -1).astype(F32)
    bf = lambda a: a.astype(BF16)

    inv_freq = ROPE_THETA ** (-jnp.arange(0, DA_HEAD_DIM, 2, dtype=F32) / DA_HEAD_DIM)
    invf = jnp.tile(inv_freq, LANES // inv_freq.shape[0]).reshape(1, LANES)
    pos = jnp.broadcast_to(positions.astype(F32).reshape(n_tok, 1), (n_tok, LANES))

    h = x.reshape(n_tok, d)
    mem2 = mem.reshape(batch * mem_len, d)
    for l in range(depth):
        lambda_init = 0.8 - 0.6 * math.exp(-0.3 * l)
        h, q, k, v, conv_o = _front(h, pos, vec(ffn1_norm[l]), bf(ffn1_w_gu[l]),
                                    bf(ffn1_w_down[l]), vec(mix_norm[l]), bf(w_in[l]), invf,
                                    conv_w[l], vec(conv_b[l]), vec(conv_ln_g[l]),
                                    vec(conv_ln_b[l]), seq=seq, qk_cols=qk_cols)
        attn_o = _attn(q, k, v, vec(lambda_q1[l]), vec(lambda_k1[l]), vec(lambda_q2[l]),
                       vec(lambda_k2[l]), vec(da_norm_g[l]), batch=batch, seq=seq,
                       lambda_init=lambda_init)
        kv = _memkv(mem2, vec(mem_norm[l]), bf(w_xkv[l]))
        last = l == depth - 1
        h = _back(h, conv_o, attn_o, bf(w_out[l]), vec(xattn_norm[l]), bf(w_xq[l]), kv,
                  bf(w_xo[l]), vec(ffn2_norm[l]), bf(ffn2_w_gu[l]), bf(ffn2_w_down[l]),
                  vec(final_norm) if last else None, seq=seq, mem_len=mem_len)
    if depth == 0:
        raise ValueError("depth must be >= 1")
    return h.reshape(batch, seq, d)
```

```python
import functools
import math

import jax
import jax.numpy as jnp
from jax import lax
from jax.experimental import pallas as pl
from jax.experimental.pallas import tpu as pltpu

F32 = jnp.float32
BF16 = jnp.bfloat16

NORM_EPS = 1e-6
NEG_INF = -1e30
ROPE_THETA = 10000.0
LOG2E = math.log2(math.e)

DA_HEADS = 4
DA_HEAD_DIM = 64
X_HEADS = 4
LANES = 128
SUBLANES = 8
ROPE_GROUPS = LANES // (DA_HEAD_DIM // 2)
CONV_HALO = 32
VMEM_LIMIT = 56 * 1024 * 1024


def _rms(x, g):
    ms = jnp.mean(x * x, axis=-1, keepdims=True)
    return x * lax.rsqrt(ms + NORM_EPS) * g


def _sigmoid(x):
    return 1.0 / (1.0 + jnp.exp(-x))


def _dot(a, b):
    return jnp.dot(a, b, preferred_element_type=F32)


def _dot_nt(a, b):
    return lax.dot_general(a, b, (((1,), (1,)), ((), ())), preferred_element_type=F32)


def _resident(shape):
    return pl.BlockSpec(shape, lambda *_: (0,) * len(shape), pipeline_mode=pl.Buffered(1))


def _params(*sem):
    return pltpu.CompilerParams(dimension_semantics=sem, vmem_limit_bytes=VMEM_LIMIT)


def _tiles(n, t):
    if n % t:
        raise ValueError(f"extent {n} is not a multiple of tile {t}")
    return n // t


def _ffn_tile(x, g_ref, wgu_ref, wd_ref, act_ref, chunks):
    d_ff = wd_ref.shape[0]
    n = _rms(x, g_ref[...]).astype(BF16)
    for c0, c1 in chunks:
        g = _dot(n, wgu_ref[:, c0:c1])
        u = _dot(n, wgu_ref[:, d_ff + c0:d_ff + c1])
        act_ref[:, c0:c1] = (g * _sigmoid(g) * u).astype(BF16)
    return x + 0.5 * _dot(act_ref[...], wd_ref[...])


def _ffn_chunks(d_ff, step=1024):
    return tuple((c, min(c + step, d_ff)) for c in range(0, d_ff, step))


def _inproj_tile(h, g, w_ref, pos_ref, invf_ref, u_ref, q_ref, k_ref, v_ref, *, q_scale):
    tm = h.shape[0]
    conv_ch = u_ref.shape[1]
    qk_cols = q_ref.shape[1]
    n = _rms(h, g).astype(BF16)
    ag = _dot(n, w_ref[:, 0:2 * conv_ch])
    u_ref[...] = ag[:, :conv_ch] * _sigmoid(ag[:, conv_ch:])
    half = DA_HEAD_DIM // 2
    rows = tm // ROPE_GROUPS
    ang = pos_ref[...] * invf_ref[...]
    cos_c = jnp.cos(ang)
    sin_c = jnp.sin(ang)
    lane = lax.broadcasted_iota(jnp.int32, (rows, LANES), 1)
    first_half = (lane % DA_HEAD_DIM) < half

    def spread(x, r):
        x = jnp.where(lane // half == r, x, 0.0)
        return sum((pltpu.roll(x, s * half, 1) for s in range(1, ROPE_GROUPS)), x)

    q0 = 2 * conv_ch
    k0 = q0 + qk_cols
    v0 = k0 + qk_cols
    q = _dot(n, w_ref[:, q0:k0])
    k = _dot(n, w_ref[:, k0:v0])
    for r in range(ROPE_GROUPS):
        cos = spread(cos_c, r)
        sin = spread(sin_c, r)
        sin = jnp.where(first_half, -sin, sin)

        def rope(x):
            partner = jnp.where(first_half, pltpu.roll(x, LANES - half, 1),
                                pltpu.roll(x, half, 1))
            return x * cos + partner * sin

        rs = slice(r * rows, (r + 1) * rows)
        for c in range(0, qk_cols, LANES):
            q_ref[rs, c:c + LANES] = (rope(q[rs, c:c + LANES]) * q_scale).astype(BF16)
            k_ref[rs, c:c + LANES] = rope(k[rs, c:c + LANES]).astype(BF16)
    v_ref[...] = _dot(n, w_ref[:, v0:]).astype(BF16)


def _conv_tile(win_ref, sh_ref, y_ref, w_ref, b_ref, lg_ref, lb_ref):
    ts, ch = y_ref.shape
    kw = w_ref.shape[0]
    base = CONV_HALO - (kw - 1)
    chunk = 32
    for c in range(0, ch, LANES):
        for r in range(1, SUBLANES):
            rows = CONV_HALO + ts - SUBLANES
            sh_ref[r, 0:rows, :] = win_ref[r:r + rows, c:c + LANES]
        for t0 in range(0, ts, chunk):
            acc = jnp.zeros((chunk, LANES), F32)
            for j in range(kw):
                r = (base + j) % SUBLANES
                a = (base + j) // SUBLANES * SUBLANES + t0
                x = win_ref[a:a + chunk, c:c + LANES] if r == 0 else sh_ref[r, a:a + chunk, :]
                acc = acc + w_ref[j:j + 1, c:c + LANES] * x
            y_ref[t0:t0 + chunk, c:c + LANES] = acc
    y = y_ref[...] + b_ref[...]
    mu = jnp.mean(y, axis=-1, keepdims=True)
    yc = y - mu
    var = jnp.mean(yc * yc, axis=-1, keepdims=True)
    z = yc * lax.rsqrt(var + NORM_EPS) * lg_ref[...] + lb_ref[...]
    return (z * _sigmoid(z)).astype(BF16)


def _front_body(x_ref, pos_ref, g1_ref, wgu_ref, wd_ref, gm_ref, win_w_ref, invf_ref,
                cw_ref, cb_ref, lg_ref, lb_ref,
                h_ref, q_ref, k_ref, v_ref, c_ref,
                act_ref, u_ref, win_ref, sh_ref, y_ref, *, chunks, q_scale, tiles_per_seq):
    i = pl.program_id(0)
    tm = x_ref.shape[0]

    @pl.when(i == 0)
    def _():
        win_ref[...] = jnp.zeros(win_ref.shape, F32)
        u_ref[...] = jnp.zeros(u_ref.shape, F32)

    seq_start = (i - 1) % tiles_per_seq == 0
    halo = win_ref[tm:tm + CONV_HALO, :]
    win_ref[0:CONV_HALO, :] = jnp.where(seq_start, jnp.zeros_like(halo), halo)
    win_ref[CONV_HALO:CONV_HALO + tm, :] = u_ref[...]
    c_ref[...] = _conv_tile(win_ref, sh_ref, y_ref, cw_ref, cb_ref, lg_ref, lb_ref)

    h = _ffn_tile(x_ref[...], g1_ref, wgu_ref, wd_ref, act_ref, chunks)
    h_ref[...] = h
    _inproj_tile(h, gm_ref[...], win_w_ref, pos_ref, invf_ref, u_ref, q_ref, k_ref, v_ref,
                 q_scale=q_scale)


def _front(x, positions, g1, w_gu, w_down, gm, w_in, conv_w, conv_b, ln_g, ln_b,
           *, seq, qk_cols, tm=512):
    n_tok, d = x.shape
    d_ff = w_down.shape[0]
    kw, conv_ch = conv_w.shape
    in_cols = w_in.shape[1]
    v_cols = in_cols - 2 * conv_ch - 2 * qk_cols
    n = _tiles(n_tok, tm)
    half = DA_HEAD_DIM // 2
    rows = _tiles(tm, ROPE_GROUPS)
    inv_freq = ROPE_THETA ** (-jnp.arange(0, DA_HEAD_DIM, 2, dtype=F32) / DA_HEAD_DIM)
    invf = jnp.tile(inv_freq, ROPE_GROUPS).reshape(1, LANES)
    pos = positions.astype(F32).reshape(n, ROPE_GROUPS, rows).transpose(0, 2, 1)
    pos = jnp.repeat(pos, half, axis=-1).reshape(n * rows, LANES)
    cur = lambda w, t=tm: pl.BlockSpec((t, w), lambda i: (jnp.minimum(i, n - 1), 0))
    lag = lambda w: pl.BlockSpec((tm, w), lambda i: (jnp.maximum(i - 1, 0), 0))
    return pl.pallas_call(
        functools.partial(_front_body, chunks=_ffn_chunks(d_ff),
                          q_scale=DA_HEAD_DIM ** -0.5 * LOG2E, tiles_per_seq=_tiles(seq, tm)),
        out_shape=(jax.ShapeDtypeStruct((n_tok, d), F32),
                   jax.ShapeDtypeStruct((n_tok, qk_cols), BF16),
                   jax.ShapeDtypeStruct((n_tok, qk_cols), BF16),
                   jax.ShapeDtypeStruct((n_tok, v_cols), BF16),
                   jax.ShapeDtypeStruct((n_tok, conv_ch), BF16)),
        grid=(n + 1,),
        in_specs=[cur(d), cur(LANES, rows), _resident((1, d)), _resident((d, 2 * d_ff)),
                  _resident((d_ff, d)), _resident((1, d)), _resident((d, in_cols)),
                  _resident((1, LANES)), _resident((kw, conv_ch)), _resident((1, conv_ch)),
                  _resident((1, conv_ch)), _resident((1, conv_ch))],
        out_specs=(cur(d), cur(qk_cols), cur(qk_cols), cur(v_cols), lag(conv_ch)),
        scratch_shapes=[pltpu.VMEM((tm, d_ff), BF16),
                        pltpu.VMEM((tm, conv_ch), F32),
                        pltpu.VMEM((CONV_HALO + tm, conv_ch), F32),
                        pltpu.VMEM((SUBLANES, CONV_HALO + tm, LANES), F32),
                        pltpu.VMEM((tm, conv_ch), F32)],
        compiler_params=_params("arbitrary"),
        name="front",
    )(x, pos, g1, w_gu, w_down, gm, w_in, invf, conv_w, conv_b, ln_g, ln_b)


def _attn_body(lq1_ref, lk1_ref, lq2_ref, lk2_ref, dag_ref, q_ref, k_ref, v_ref, o_ref,
               *, lambda_init, tq):
    seq = q_ref.shape[0]
    tk = tq
    nq = seq // tq
    lane = lax.broadcasted_iota(jnp.int32, (tq, LANES), 1)
    key_le_query = (lax.broadcasted_iota(jnp.int32, (tk, tq), 0)
                    <= lax.broadcasted_iota(jnp.int32, (tk, tq), 1))
    lam = (jnp.exp(jnp.sum(lq1_ref[...] * lk1_ref[...], axis=-1, keepdims=True))
           - jnp.exp(jnp.sum(lq2_ref[...] * lk2_ref[...], axis=-1, keepdims=True))
           + lambda_init)
    vt = v_ref[...].astype(F32).T.astype(BF16)

    def scores(i):
        q = q_ref[i * tq:(i + 1) * tq, :]
        zero = jnp.zeros_like(q)
        qs = (jnp.where(lane < DA_HEAD_DIM, q, zero), jnp.where(lane >= DA_HEAD_DIM, q, zero))
        keys = k_ref[0:(i + 1) * tk, :]
        return [_dot_nt(keys, qz) for qz in qs]

    def probs(st):
        n_full = st.shape[0] - tk
        diag = jnp.where(key_le_query, st[n_full:], NEG_INF)
        m = jnp.max(diag, axis=0, keepdims=True)
        if n_full:
            full = st[:n_full]
            m = jnp.maximum(m, jnp.max(full, axis=0, keepdims=True))
            parts = [jnp.exp2(full - m), jnp.exp2(diag - m)]
        else:
            parts = [jnp.exp2(diag - m)]
        return parts, sum(jnp.sum(p, axis=0, keepdims=True) for p in parts)

    st_next = scores(0)
    for i in range(nq):
        st = st_next
        if i + 1 < nq:
            st_next = scores(i + 1)
        p1, l1 = probs(st[0])
        p2, l2 = probs(st[1])
        rho = lam * l1 * (1.0 / l2)
        a = jnp.concatenate([(x - rho * y).astype(BF16) for x, y in zip(p1, p2)], axis=0)
        o = (_dot(vt[:, 0:(i + 1) * tk], a) * (1.0 / l1)).T
        o_ref[i * tq:(i + 1) * tq, :] = (_rms(o, dag_ref[...]) * (1.0 - lambda_init)).astype(BF16)


def _attn(q, k, v, lq1, lk1, lq2, lk2, da_g, *, batch, seq, lambda_init, tq=256):
    n_tok, cols = q.shape
    heads = cols // LANES
    _tiles(seq, tq)
    blk = pl.BlockSpec((seq, LANES), lambda b, h: (b, h))
    vec = lambda w: pl.BlockSpec((1, w), lambda b, h: (0, 0))
    return pl.pallas_call(
        functools.partial(_attn_body, lambda_init=lambda_init, tq=tq),
        out_shape=jax.ShapeDtypeStruct((n_tok, cols), BF16),
        grid=(batch, heads),
        in_specs=[vec(DA_HEAD_DIM)] * 4 + [vec(LANES), blk, blk, blk],
        out_specs=blk,
        compiler_params=_params("parallel", "parallel"),
        name="diff_attn",
    )(lq1, lk1, lq2, lk2, da_g, q, k, v)


def _memkv_body(m_ref, g_ref, w_ref, o_ref):
    n = _rms(m_ref[...], g_ref[...]).astype(BF16)
    o_ref[...] = _dot(n, w_ref[...]).astype(BF16)


def _memkv(mem, norm_g, w_xkv, *, tm=512):
    n_mem, d = mem.shape
    cols = w_xkv.shape[1]
    return pl.pallas_call(
        _memkv_body,
        out_shape=jax.ShapeDtypeStruct((n_mem, cols), BF16),
        grid=(_tiles(n_mem, tm),),
        in_specs=[pl.BlockSpec((tm, d), lambda i: (i, 0)), _resident((1, d)),
                  _resident((d, cols))],
        out_specs=pl.BlockSpec((tm, cols), lambda i: (i, 0)),
        compiler_params=_params("parallel"),
        name="memkv",
    )(mem, norm_g, w_xkv)


def _back_body(*refs, x_scale, chunks, final):
    (h_ref, c_ref, a_ref, wout_ref, xg_ref, wxq_ref, kv_ref, wxo_ref,
     g2_ref, wgu_ref, wd_ref) = refs[:11]
    fin_ref = refs[11] if final else None
    o_ref, oc_ref, act_ref = refs[-3:]
    d = h_ref.shape[1]
    conv_ch = c_ref.shape[1]
    hd = d // X_HEADS
    h2 = (h_ref[...] + _dot(c_ref[...], wout_ref[0:conv_ch, :])
          + _dot(a_ref[...], wout_ref[conv_ch:, :]))
    n = _rms(h2, xg_ref[...]).astype(BF16)
    q = (_dot(n, wxq_ref[...]) * x_scale).astype(BF16)
    for hh in range(X_HEADS):
        qh = q[:, hh * hd:(hh + 1) * hd]
        s = _dot_nt(qh, kv_ref[:, hh * hd:(hh + 1) * hd])
        p = jnp.exp2(s - jnp.max(s, axis=-1, keepdims=True))
        inv = 1.0 / jnp.sum(p, axis=-1, keepdims=True)
        o = _dot(p.astype(BF16), kv_ref[:, d + hh * hd:d + (hh + 1) * hd]) * inv
        oc_ref[:, hh * hd:(hh + 1) * hd] = o.astype(BF16)
    h3 = h2 + _dot(oc_ref[...], wxo_ref[...])
    y = _ffn_tile(h3, g2_ref, wgu_ref, wd_ref, act_ref, chunks)
    if final:
        y = _rms(y, fin_ref[...])
    o_ref[...] = y


def _back(h, conv_o, attn_o, w_out, xg, w_xq, kv, w_xo, g2, w_gu, w_down, final_g=None,
          *, seq, mem_len, tm=512):
    n_tok, d = h.shape
    ch = conv_o.shape[1]
    d_ff = w_down.shape[0]
    per_b = _tiles(seq, tm)
    x_scale = (d // X_HEADS) ** -0.5 * LOG2E
    final = final_g is not None
    row = lambda w: pl.BlockSpec((tm, w), lambda i: (i, 0))
    in_specs = [row(d), row(ch), row(attn_o.shape[1]), _resident(w_out.shape),
                _resident((1, d)), _resident(w_xq.shape),
                pl.BlockSpec((mem_len, 2 * d), lambda i: (i // per_b, 0)),
                _resident(w_xo.shape), _resident((1, d)), _resident((d, 2 * d_ff)),
                _resident((d_ff, d))]
    args = [h, conv_o, attn_o, w_out, xg, w_xq, kv, w_xo, g2, w_gu, w_down]
    if final:
        in_specs.append(_resident((1, d)))
        args.append(final_g)
    return pl.pallas_call(
        functools.partial(_back_body, x_scale=x_scale, chunks=_ffn_chunks(d_ff), final=final),
        out_shape=jax.ShapeDtypeStruct((n_tok, d), F32),
        grid=(_tiles(n_tok, tm),),
        in_specs=in_specs,
        out_specs=row(d),
        scratch_shapes=[pltpu.VMEM((tm, d), BF16), pltpu.VMEM((tm, d_ff), BF16)],
        compiler_params=_params("parallel"),
        name="back",
    )(*args)


def kernel(x, mem, positions, ffn1_norm, ffn1_w_gu, ffn1_w_down, mix_norm, w_in, conv_w, conv_b,
           conv_ln_g, conv_ln_b, lambda_q1, lambda_k1, lambda_q2, lambda_k2, da_norm_g, w_out,
           xattn_norm, mem_norm, w_xq, w_xkv, w_xo, ffn2_norm, ffn2_w_gu, ffn2_w_down,
           final_norm):
    batch, seq, d = x.shape
    mem_len = mem.shape[1]
    depth = ffn1_norm.shape[0]
    qk_cols = DA_HEADS * 2 * DA_HEAD_DIM
    n_tok = batch * seq

    vec = lambda a: a.reshape(1, -1).astype(F32)
    bf = lambda a: a.astype(BF16)

    pos = positions.reshape(n_tok)
    h = x.reshape(n_tok, d)
    mem2 = mem.reshape(batch * mem_len, d)
    for l in range(depth):
        lambda_init = 0.8 - 0.6 * math.exp(-0.3 * l)
        h, q, k, v, conv_o = _front(h, pos, vec(ffn1_norm[l]), bf(ffn1_w_gu[l]),
                                    bf(ffn1_w_down[l]), vec(mix_norm[l]), bf(w_in[l]),
                                    conv_w[l], vec(conv_b[l]), vec(conv_ln_g[l]),
                                    vec(conv_ln_b[l]), seq=seq, qk_cols=qk_cols)
        attn_o = _attn(q, k, v, vec(lambda_q1[l]), vec(lambda_k1[l]), vec(lambda_q2[l]),
                       vec(lambda_k2[l]), vec(da_norm_g[l]), batch=batch, seq=seq,
                       lambda_init=lambda_init)
        kv = _memkv(mem2, vec(mem_norm[l]), bf(w_xkv[l]))
        last = l == depth - 1
        h = _back(h, conv_o, attn_o, bf(w_out[l]), vec(xattn_norm[l]), bf(w_xq[l]), kv,
                  bf(w_xo[l]), vec(ffn2_norm[l]), bf(ffn2_w_gu[l]), bf(ffn2_w_down[l]),
                  vec(final_norm) if last else None, seq=seq, mem_len=mem_len)
    if depth == 0:
        raise ValueError("depth must be >= 1")
    return h.reshape(batch, seq, d)
```

```python
import functools
import math

import jax
import jax.numpy as jnp
from jax import lax
from jax.experimental import pallas as pl
from jax.experimental.pallas import tpu as pltpu

F32 = jnp.float32
BF16 = jnp.bfloat16

NORM_EPS = 1e-6
NEG_INF = -1e30
ROPE_THETA = 10000.0
LOG2E = math.log2(math.e)

DA_HEADS = 4
DA_HEAD_DIM = 64
X_HEADS = 4
LANES = 128
SUBLANES = 8
ROPE_GROUPS = LANES // (DA_HEAD_DIM // 2)
CONV_HALO = 32
VMEM_LIMIT = 56 * 1024 * 1024


def _rms(x, g):
    ms = jnp.mean(x * x, axis=-1, keepdims=True)
    return x * lax.rsqrt(ms + NORM_EPS) * g


def _sigmoid(x):
    return 1.0 / (1.0 + jnp.exp(-x))


def _dot(a, b):
    return jnp.dot(a, b, preferred_element_type=F32)


def _dot_nt(a, b):
    return lax.dot_general(a, b, (((1,), (1,)), ((), ())), preferred_element_type=F32)


def _resident(shape):
    return pl.BlockSpec(shape, lambda *_: (0,) * len(shape), pipeline_mode=pl.Buffered(1))


def _params(*sem):
    return pltpu.CompilerParams(dimension_semantics=sem, vmem_limit_bytes=VMEM_LIMIT)


def _tiles(n, t):
    if n % t:
        raise ValueError(f"extent {n} is not a multiple of tile {t}")
    return n // t


def _ffn_tile(x, g_ref, wgu_ref, wd_ref, act_ref, chunks, side=()):
    side = list(side)
    d_ff, d = wd_ref.shape
    if not side:
        n = _rms(x, g_ref[...]).astype(BF16)
        for c0, c1 in chunks:
            g = _dot(n, wgu_ref[:, c0:c1])
            u = _dot(n, wgu_ref[:, d_ff + c0:d_ff + c1])
            act_ref[:, c0:c1] = (g * _sigmoid(g) * u).astype(BF16)
        return x + 0.5 * _dot(act_ref[...], wd_ref[...])

    step = chunks[0][1] - chunks[0][0]
    pieces_left = [len(chunks) + d // step]

    def tied(val):
        take = len(side) // pieces_left[0]
        pieces_left[0] -= 1
        for _ in range(take):
            zero = side.pop(0)()
            val = val + jnp.concatenate([zero] * (val.shape[1] // LANES), axis=1)
        return val

    n = _rms(x, g_ref[...]).astype(BF16)
    for c0, c1 in chunks:
        g = _dot(n, wgu_ref[:, c0:c1])
        u = _dot(n, wgu_ref[:, d_ff + c0:d_ff + c1])
        act_ref[:, c0:c1] = tied(g * _sigmoid(g) * u).astype(BF16)
    outs = [tied(x[:, c:c + step] + 0.5 * _dot(act_ref[...], wd_ref[:, c:c + step]))
            for c in range(0, d, step)]
    return jnp.concatenate(outs, axis=1)


def _ffn_chunks(d_ff, step=1024):
    return tuple((c, min(c + step, d_ff)) for c in range(0, d_ff, step))


def _inproj_tile(h, g, w_ref, pos_ref, invf_ref, u_ref, q_ref, k_ref, v_ref, *, q_scale):
    tm = h.shape[0]
    conv_ch = u_ref.shape[1]
    qk_cols = q_ref.shape[1]
    n = _rms(h, g).astype(BF16)
    ag = _dot(n, w_ref[:, 0:2 * conv_ch])
    u_ref[...] = ag[:, :conv_ch] * _sigmoid(ag[:, conv_ch:])
    half = DA_HEAD_DIM // 2
    rows = tm // ROPE_GROUPS
    ang = pos_ref[...] * invf_ref[...]
    cos_c = jnp.cos(ang)
    sin_c = jnp.sin(ang)
    lane = lax.broadcasted_iota(jnp.int32, (rows, LANES), 1)
    first_half = (lane % DA_HEAD_DIM) < half

    def spread(x, r):
        x = jnp.where(lane // half == r, x, 0.0)
        return sum((pltpu.roll(x, s * half, 1) for s in range(1, ROPE_GROUPS)), x)

    q0 = 2 * conv_ch
    k0 = q0 + qk_cols
    v0 = k0 + qk_cols
    q = _dot(n, w_ref[:, q0:k0])
    k = _dot(n, w_ref[:, k0:v0])
    for r in range(ROPE_GROUPS):
        cos = spread(cos_c, r)
        sin = spread(sin_c, r)
        sin = jnp.where(first_half, -sin, sin)

        def rope(x):
            partner = jnp.where(first_half, pltpu.roll(x, LANES - half, 1),
                                pltpu.roll(x, half, 1))
            return x * cos + partner * sin

        rs = slice(r * rows, (r + 1) * rows)
        for c in range(0, qk_cols, LANES):
            q_ref[rs, c:c + LANES] = (rope(q[rs, c:c + LANES]) * q_scale).astype(BF16)
            k_ref[rs, c:c + LANES] = rope(k[rs, c:c + LANES]).astype(BF16)
    v_ref[...] = _dot(n, w_ref[:, v0:]).astype(BF16)


def _zero_row_after(x):
    bits = pltpu.bitcast(x, jnp.uint32)
    while bits.shape[0] > SUBLANES:
        half = bits.shape[0] // 2
        bits = bits[:half] | bits[half:]
    return pltpu.bitcast((bits >> 16) >> 16, F32)[0:1, :]


def _conv_pieces(win_ref, sh_ref, y_ref, w_ref, *, block=128):
    ts, ch = y_ref.shape
    kw = w_ref.shape[0]
    base = CONV_HALO - (kw - 1)
    chunk = 32

    def piece(c, b0):
        if b0 == 0:
            for r in range(1, SUBLANES):
                rows = CONV_HALO + ts - SUBLANES
                sh_ref[r, 0:rows, :] = win_ref[r:r + rows, c:c + LANES]
        zero = None
        for t0 in range(b0, b0 + block, chunk):
            acc = jnp.zeros((chunk, LANES), F32)
            for j in range(kw):
                r = (base + j) % SUBLANES
                a = (base + j) // SUBLANES * SUBLANES + t0
                x = win_ref[a:a + chunk, c:c + LANES] if r == 0 else sh_ref[r, a:a + chunk, :]
                acc = acc + w_ref[j:j + 1, c:c + LANES] * x
            y_ref[t0:t0 + chunk, c:c + LANES] = acc
            z = _zero_row_after(acc)
            zero = z if zero is None else zero + z
        return zero

    return [functools.partial(piece, c, b0)
            for c in range(0, ch, LANES) for b0 in range(0, ts, block)]


def _conv_finish(y_ref, b_ref, lg_ref, lb_ref):
    y = y_ref[...] + b_ref[...]
    mu = jnp.mean(y, axis=-1, keepdims=True)
    yc = y - mu
    var = jnp.mean(yc * yc, axis=-1, keepdims=True)
    z = yc * lax.rsqrt(var + NORM_EPS) * lg_ref[...] + lb_ref[...]
    return (z * _sigmoid(z)).astype(BF16)


def _front_body(x_ref, pos_ref, g1_ref, wgu_ref, wd_ref, gm_ref, win_w_ref, invf_ref,
                cw_ref, cb_ref, lg_ref, lb_ref,
                h_ref, q_ref, k_ref, v_ref, c_ref,
                act_ref, u_ref, win_ref, sh_ref, y_ref, *, chunks, q_scale, tiles_per_seq):
    i = pl.program_id(0)
    tm = x_ref.shape[0]

    @pl.when(i == 0)
    def _():
        win_ref[...] = jnp.zeros(win_ref.shape, F32)
        u_ref[...] = jnp.zeros(u_ref.shape, F32)

    seq_start = (i - 1) % tiles_per_seq == 0
    halo = win_ref[tm:tm + CONV_HALO, :]
    win_ref[0:CONV_HALO, :] = jnp.where(seq_start, jnp.zeros_like(halo), halo)
    win_ref[CONV_HALO:CONV_HALO + tm, :] = u_ref[...]
    h = _ffn_tile(x_ref[...], g1_ref, wgu_ref, wd_ref, act_ref, chunks,
                  side=_conv_pieces(win_ref, sh_ref, y_ref, cw_ref))
    h_ref[...] = h
    c_ref[...] = _conv_finish(y_ref, cb_ref, lg_ref, lb_ref)
    _inproj_tile(h, gm_ref[...], win_w_ref, pos_ref, invf_ref, u_ref, q_ref, k_ref, v_ref,
                 q_scale=q_scale)


def _front(x, positions, g1, w_gu, w_down, gm, w_in, conv_w, conv_b, ln_g, ln_b,
           *, seq, qk_cols, tm=512):
    n_tok, d = x.shape
    d_ff = w_down.shape[0]
    kw, conv_ch = conv_w.shape
    in_cols = w_in.shape[1]
    v_cols = in_cols - 2 * conv_ch - 2 * qk_cols
    n = _tiles(n_tok, tm)
    half = DA_HEAD_DIM // 2
    rows = _tiles(tm, ROPE_GROUPS)
    inv_freq = ROPE_THETA ** (-jnp.arange(0, DA_HEAD_DIM, 2, dtype=F32) / DA_HEAD_DIM)
    invf = jnp.tile(inv_freq, ROPE_GROUPS).reshape(1, LANES)
    pos = positions.astype(F32).reshape(n, ROPE_GROUPS, rows).transpose(0, 2, 1)
    pos = jnp.repeat(pos, half, axis=-1).reshape(n * rows, LANES)
    cur = lambda w, t=tm: pl.BlockSpec((t, w), lambda i: (jnp.minimum(i, n - 1), 0))
    lag = lambda w: pl.BlockSpec((tm, w), lambda i: (jnp.maximum(i - 1, 0), 0))
    return pl.pallas_call(
        functools.partial(_front_body, chunks=_ffn_chunks(d_ff, 256),
                          q_scale=DA_HEAD_DIM ** -0.5 * LOG2E, tiles_per_seq=_tiles(seq, tm)),
        out_shape=(jax.ShapeDtypeStruct((n_tok, d), F32),
                   jax.ShapeDtypeStruct((n_tok, qk_cols), BF16),
                   jax.ShapeDtypeStruct((n_tok, qk_cols), BF16),
                   jax.ShapeDtypeStruct((n_tok, v_cols), BF16),
                   jax.ShapeDtypeStruct((n_tok, conv_ch), BF16)),
        grid=(n + 1,),
        in_specs=[cur(d), cur(LANES, rows), _resident((1, d)), _resident((d, 2 * d_ff)),
                  _resident((d_ff, d)), _resident((1, d)), _resident((d, in_cols)),
                  _resident((1, LANES)), _resident((kw, conv_ch)), _resident((1, conv_ch)),
                  _resident((1, conv_ch)), _resident((1, conv_ch))],
        out_specs=(cur(d), cur(qk_cols), cur(qk_cols), cur(v_cols), lag(conv_ch)),
        scratch_shapes=[pltpu.VMEM((tm, d_ff), BF16),
                        pltpu.VMEM((tm, conv_ch), F32),
                        pltpu.VMEM((CONV_HALO + tm, conv_ch), F32),
                        pltpu.VMEM((SUBLANES, CONV_HALO + tm, LANES), F32),
                        pltpu.VMEM((tm, conv_ch), F32)],
        compiler_params=_params("arbitrary"),
        name="front",
    )(x, pos, g1, w_gu, w_down, gm, w_in, invf, conv_w, conv_b, ln_g, ln_b)


def _attn_body(lq1_ref, lk1_ref, lq2_ref, lk2_ref, dag_ref, q_ref, k_ref, v_ref, o_ref,
               *, lambda_init, tq):
    seq = q_ref.shape[0]
    tk = tq
    nq = seq // tq
    lane = lax.broadcasted_iota(jnp.int32, (tq, LANES), 1)
    key_le_query = (lax.broadcasted_iota(jnp.int32, (tk, tq), 0)
                    <= lax.broadcasted_iota(jnp.int32, (tk, tq), 1))
    lam = (jnp.exp(jnp.sum(lq1_ref[...] * lk1_ref[...], axis=-1, keepdims=True))
           - jnp.exp(jnp.sum(lq2_ref[...] * lk2_ref[...], axis=-1, keepdims=True))
           + lambda_init)
    vt = v_ref[...].astype(F32).T.astype(BF16)

    def scores(i):
        q = q_ref[i * tq:(i + 1) * tq, :]
        zero = jnp.zeros_like(q)
        qs = (jnp.where(lane < DA_HEAD_DIM, q, zero), jnp.where(lane >= DA_HEAD_DIM, q, zero))
        keys = k_ref[0:(i + 1) * tk, :]
        return [_dot_nt(keys, qz) for qz in qs]

    def probs(st):
        n_full = st.shape[0] - tk
        diag = jnp.where(key_le_query, st[n_full:], NEG_INF)
        m = jnp.max(diag, axis=0, keepdims=True)
        if n_full:
            full = st[:n_full]
            m = jnp.maximum(m, jnp.max(full, axis=0, keepdims=True))
            parts = [jnp.exp2(full - m), jnp.exp2(diag - m)]
        else:
            parts = [jnp.exp2(diag - m)]
        return parts, sum(jnp.sum(p, axis=0, keepdims=True) for p in parts)

    st_next = scores(0)
    for i in range(nq):
        st = st_next
        if i + 1 < nq:
            st_next = scores(i + 1)
        p1, l1 = probs(st[0])
        p2, l2 = probs(st[1])
        rho = lam * l1 * (1.0 / l2)
        a = jnp.concatenate([(x - rho * y).astype(BF16) for x, y in zip(p1, p2)], axis=0)
        o = (_dot(vt[:, 0:(i + 1) * tk], a) * (1.0 / l1)).T
        o_ref[i * tq:(i + 1) * tq, :] = (_rms(o, dag_ref[...]) * (1.0 - lambda_init)).astype(BF16)


def _attn(q, k, v, lq1, lk1, lq2, lk2, da_g, *, batch, seq, lambda_init, tq=256):
    n_tok, cols = q.shape
    heads = cols // LANES
    _tiles(seq, tq)
    blk = pl.BlockSpec((seq, LANES), lambda b, h: (b, h))
    vec = lambda w: pl.BlockSpec((1, w), lambda b, h: (0, 0))
    return pl.pallas_call(
        functools.partial(_attn_body, lambda_init=lambda_init, tq=tq),
        out_shape=jax.ShapeDtypeStruct((n_tok, cols), BF16),
        grid=(batch, heads),
        in_specs=[vec(DA_HEAD_DIM)] * 4 + [vec(LANES), blk, blk, blk],
        out_specs=blk,
        compiler_params=_params("parallel", "parallel"),
        name="diff_attn",
    )(lq1, lk1, lq2, lk2, da_g, q, k, v)


def _memkv_body(m_ref, g_ref, w_ref, o_ref):
    n = _rms(m_ref[...], g_ref[...]).astype(BF16)
    o_ref[...] = _dot(n, w_ref[...]).astype(BF16)


def _memkv(mem, norm_g, w_xkv, *, tm=512):
    n_mem, d = mem.shape
    cols = w_xkv.shape[1]
    return pl.pallas_call(
        _memkv_body,
        out_shape=jax.ShapeDtypeStruct((n_mem, cols), BF16),
        grid=(_tiles(n_mem, tm),),
        in_specs=[pl.BlockSpec((tm, d), lambda i: (i, 0)), _resident((1, d)),
                  _resident((d, cols))],
        out_specs=pl.BlockSpec((tm, cols), lambda i: (i, 0)),
        compiler_params=_params("parallel"),
        name="memkv",
    )(mem, norm_g, w_xkv)


def _back_body(*refs, x_scale, chunks, final):
    (h_ref, c_ref, a_ref, wout_ref, xg_ref, wxq_ref, kv_ref, wxo_ref,
     g2_ref, wgu_ref, wd_ref) = refs[:11]
    fin_ref = refs[11] if final else None
    o_ref, oc_ref, act_ref = refs[-3:]
    d = h_ref.shape[1]
    conv_ch = c_ref.shape[1]
    hd = d // X_HEADS
    h2 = (h_ref[...] + _dot(c_ref[...], wout_ref[0:conv_ch, :])
          + _dot(a_ref[...], wout_ref[conv_ch:, :]))
    n = _rms(h2, xg_ref[...]).astype(BF16)
    q = (_dot(n, wxq_ref[...]) * x_scale).astype(BF16)
    for hh in range(X_HEADS):
        qh = q[:, hh * hd:(hh + 1) * hd]
        s = _dot_nt(qh, kv_ref[:, hh * hd:(hh + 1) * hd])
        p = jnp.exp2(s - jnp.max(s, axis=-1, keepdims=True))
        inv = 1.0 / jnp.sum(p, axis=-1, keepdims=True)
        o = _dot(p.astype(BF16), kv_ref[:, d + hh * hd:d + (hh + 1) * hd]) * inv
        oc_ref[:, hh * hd:(hh + 1) * hd] = o.astype(BF16)
    h3 = h2 + _dot(oc_ref[...], wxo_ref[...])
    y = _ffn_tile(h3, g2_ref, wgu_ref, wd_ref, act_ref, chunks)
    if final:
        y = _rms(y, fin_ref[...])
    o_ref[...] = y


def _back(h, conv_o, attn_o, w_out, xg, w_xq, kv, w_xo, g2, w_gu, w_down, final_g=None,
          *, seq, mem_len, tm=512):
    n_tok, d = h.shape
    ch = conv_o.shape[1]
    d_ff = w_down.shape[0]
    per_b = _tiles(seq, tm)
    x_scale = (d // X_HEADS) ** -0.5 * LOG2E
    final = final_g is not None
    row = lambda w: pl.BlockSpec((tm, w), lambda i: (i, 0))
    in_specs = [row(d), row(ch), row(attn_o.shape[1]), _resident(w_out.shape),
                _resident((1, d)), _resident(w_xq.shape),
                pl.BlockSpec((mem_len, 2 * d), lambda i: (i // per_b, 0)),
                _resident(w_xo.shape), _resident((1, d)), _resident((d, 2 * d_ff)),
                _resident((d_ff, d))]
    args = [h, conv_o, attn_o, w_out, xg, w_xq, kv, w_xo, g2, w_gu, w_down]
    if final:
        in_specs.append(_resident((1, d)))
        args.append(final_g)
    return pl.pallas_call(
        functools.partial(_back_body, x_scale=x_scale, chunks=_ffn_chunks(d_ff), final=final),
        out_shape=jax.ShapeDtypeStruct((n_tok, d), F32),
        grid=(_tiles(n_tok, tm),),
        in_specs=in_specs,
        out_specs=row(d),
        scratch_shapes=[pltpu.VMEM((tm, d), BF16), pltpu.VMEM((tm, d_ff), BF16)],
        compiler_params=_params("parallel"),
        name="back",
    )(*args)


def kernel(x, mem, positions, ffn1_norm, ffn1_w_gu, ffn1_w_down, mix_norm, w_in, conv_w, conv_b,
           conv_ln_g, conv_ln_b, lambda_q1, lambda_k1, lambda_q2, lambda_k2, da_norm_g, w_out,
           xattn_norm, mem_norm, w_xq, w_xkv, w_xo, ffn2_norm, ffn2_w_gu, ffn2_w_down,
           final_norm):
    batch, seq, d = x.shape
    mem_len = mem.shape[1]
    depth = ffn1_norm.shape[0]
    qk_cols = DA_HEADS * 2 * DA_HEAD_DIM
    n_tok = batch * seq

    vec = lambda a: a.reshape(1, -1).astype(F32)
    bf = lambda a: a.astype(BF16)

    pos = positions.reshape(n_tok)
    h = x.reshape(n_tok, d)
    mem2 = mem.reshape(batch * mem_len, d)
    for l in range(depth):
        lambda_init = 0.8 - 0.6 * math.exp(-0.3 * l)
        h, q, k, v, conv_o = _front(h, pos, vec(ffn1_norm[l]), bf(ffn1_w_gu[l]),
                                    bf(ffn1_w_down[l]), vec(mix_norm[l]), bf(w_in[l]),
                                    conv_w[l], vec(conv_b[l]), vec(conv_ln_g[l]),
                                    vec(conv_ln_b[l]), seq=seq, qk_cols=qk_cols)
        attn_o = _attn(q, k, v, vec(lambda_q1[l]), vec(lambda_k1[l]), vec(lambda_q2[l]),
                       vec(lambda_k2[l]), vec(da_norm_g[l]), batch=batch, seq=seq,
                       lambda_init=lambda_init)
        kv = _memkv(mem2, vec(mem_norm[l]), bf(w_xkv[l]))
        last = l == depth - 1
        h = _back(h, conv_o, attn_o, bf(w_out[l]), vec(xattn_norm[l]), bf(w_xq[l]), kv,
                  bf(w_xo[l]), vec(ffn2_norm[l]), bf(ffn2_w_gu[l]), bf(ffn2_w_down[l]),
                  vec(final_norm) if last else None, seq=seq, mem_len=mem_len)
    if depth == 0:
        raise ValueError("depth must be >= 1")
    return h.reshape(batch, seq, d)
```

```python
import functools
import math

import jax
import jax.numpy as jnp
from jax import lax
from jax.experimental import pallas as pl
from jax.experimental.pallas import tpu as pltpu

F32 = jnp.float32
BF16 = jnp.bfloat16

NORM_EPS = 1e-6
NEG_INF = -1e30
ROPE_THETA = 10000.0
LOG2E = math.log2(math.e)

DA_HEADS = 4
DA_HEAD_DIM = 64
X_HEADS = 4
LANES = 128
SUBLANES = 8
ROPE_GROUPS = LANES // (DA_HEAD_DIM // 2)
CONV_HALO = 32
VMEM_LIMIT = 56 * 1024 * 1024


def _rms(x, g):
    ms = jnp.mean(x * x, axis=-1, keepdims=True)
    return x * lax.rsqrt(ms + NORM_EPS) * g


def _sigmoid(x):
    return 1.0 / (1.0 + jnp.exp(-x))


def _dot(a, b):
    return jnp.dot(a, b, preferred_element_type=F32)


def _dot_nt(a, b):
    return lax.dot_general(a, b, (((1,), (1,)), ((), ())), preferred_element_type=F32)


def _resident(shape):
    return pl.BlockSpec(shape, lambda *_: (0,) * len(shape), pipeline_mode=pl.Buffered(1))


def _params(*sem):
    return pltpu.CompilerParams(dimension_semantics=sem, vmem_limit_bytes=VMEM_LIMIT)


def _tiles(n, t):
    if n % t:
        raise ValueError(f"extent {n} is not a multiple of tile {t}")
    return n // t


def _ffn_tile(x, g_ref, wgu_ref, wd_ref, act_ref, chunks, side=()):
    side = list(side)
    d_ff, d = wd_ref.shape
    if not side:
        n = _rms(x, g_ref[...]).astype(BF16)
        for c0, c1 in chunks:
            g = _dot(n, wgu_ref[:, c0:c1])
            u = _dot(n, wgu_ref[:, d_ff + c0:d_ff + c1])
            act_ref[:, c0:c1] = (g * _sigmoid(g) * u).astype(BF16)
        return x + 0.5 * _dot(act_ref[...], wd_ref[...])

    step = chunks[0][1] - chunks[0][0]
    pieces_left = [len(chunks) + d // step]

    def tied(val):
        take = len(side) // pieces_left[0]
        pieces_left[0] -= 1
        for _ in range(take):
            zero = side.pop(0)()
            val = val + jnp.concatenate([zero] * (val.shape[1] // LANES), axis=1)
        return val

    n = _rms(x, g_ref[...]).astype(BF16)
    for c0, c1 in chunks:
        g = _dot(n, wgu_ref[:, c0:c1])
        u = _dot(n, wgu_ref[:, d_ff + c0:d_ff + c1])
        act_ref[:, c0:c1] = tied(g * _sigmoid(g) * u).astype(BF16)
    outs = [tied(x[:, c:c + step] + 0.5 * _dot(act_ref[...], wd_ref[:, c:c + step]))
            for c in range(0, d, step)]
    return jnp.concatenate(outs, axis=1)


def _ffn_chunks(d_ff, step=1024):
    return tuple((c, min(c + step, d_ff)) for c in range(0, d_ff, step))


def _inproj_tile(h, g, w_ref, pos_ref, invf_ref, u_ref, q_ref, k_ref, v_ref, *, q_scale):
    tm = h.shape[0]
    conv_ch = u_ref.shape[1]
    qk_cols = q_ref.shape[1]
    n = _rms(h, g).astype(BF16)
    ag = _dot(n, w_ref[:, 0:2 * conv_ch])
    u_ref[...] = ag[:, :conv_ch] * _sigmoid(ag[:, conv_ch:])
    half = DA_HEAD_DIM // 2
    rows = tm // ROPE_GROUPS
    ang = pos_ref[...] * invf_ref[...]
    cos_c = jnp.cos(ang)
    sin_c = jnp.sin(ang)
    lane = lax.broadcasted_iota(jnp.int32, (rows, LANES), 1)
    first_half = (lane % DA_HEAD_DIM) < half

    def spread(x, r):
        x = jnp.where(lane // half == r, x, 0.0)
        return sum((pltpu.roll(x, s * half, 1) for s in range(1, ROPE_GROUPS)), x)

    q0 = 2 * conv_ch
    k0 = q0 + qk_cols
    v0 = k0 + qk_cols
    q = _dot(n, w_ref[:, q0:k0])
    k = _dot(n, w_ref[:, k0:v0])
    for r in range(ROPE_GROUPS):
        cos = spread(cos_c, r)
        sin = spread(sin_c, r)
        sin = jnp.where(first_half, -sin, sin)

        def rope(x):
            partner = jnp.where(first_half, pltpu.roll(x, LANES - half, 1),
                                pltpu.roll(x, half, 1))
            return x * cos + partner * sin

        rs = slice(r * rows, (r + 1) * rows)
        for c in range(0, qk_cols, LANES):
            q_ref[rs, c:c + LANES] = (rope(q[rs, c:c + LANES]) * q_scale).astype(BF16)
            k_ref[rs, c:c + LANES] = rope(k[rs, c:c + LANES]).astype(BF16)
    v_ref[...] = _dot(n, w_ref[:, v0:]).astype(BF16)


def _zero_row_after(x):
    bits = pltpu.bitcast(x, jnp.uint32)
    while bits.shape[0] > SUBLANES:
        half = bits.shape[0] // 2
        bits = bits[:half] | bits[half:]
    return pltpu.bitcast((bits >> 16) >> 16, F32)[0:1, :]


def _conv_pieces(win_ref, sh_ref, y_ref, w_ref, *, block=128):
    ts, ch = y_ref.shape
    kw = w_ref.shape[0]
    base = CONV_HALO - (kw - 1)
    chunk = 32

    def piece(c, b0):
        if b0 == 0:
            for r in range(1, SUBLANES):
                rows = CONV_HALO + ts - SUBLANES
                sh_ref[r, 0:rows, :] = win_ref[r:r + rows, c:c + LANES]
        zero = None
        for t0 in range(b0, b0 + block, chunk):
            acc = jnp.zeros((chunk, LANES), F32)
            for j in range(kw):
                r = (base + j) % SUBLANES
                a = (base + j) // SUBLANES * SUBLANES + t0
                x = win_ref[a:a + chunk, c:c + LANES] if r == 0 else sh_ref[r, a:a + chunk, :]
                acc = acc + w_ref[j:j + 1, c:c + LANES] * x
            y_ref[t0:t0 + chunk, c:c + LANES] = acc
            z = _zero_row_after(acc)
            zero = z if zero is None else zero + z
        return zero

    return [functools.partial(piece, c, b0)
            for c in range(0, ch, LANES) for b0 in range(0, ts, block)]


def _conv_finish(y_ref, b_ref, lg_ref, lb_ref):
    y = y_ref[...] + b_ref[...]
    mu = jnp.mean(y, axis=-1, keepdims=True)
    yc = y - mu
    var = jnp.mean(yc * yc, axis=-1, keepdims=True)
    z = yc * lax.rsqrt(var + NORM_EPS) * lg_ref[...] + lb_ref[...]
    return (z * _sigmoid(z)).astype(BF16)


def _front_body(x_ref, pos_ref, g1_ref, wgu_ref, wd_ref, gm_ref, win_w_ref, invf_ref,
                cw_ref, cb_ref, lg_ref, lb_ref,
                h_ref, q_ref, k_ref, v_ref, c_ref,
                act_ref, u_ref, win_ref, sh_ref, y_ref, *, chunks, q_scale, tiles_per_seq):
    i = pl.program_id(0)
    tm = x_ref.shape[0]

    @pl.when(i == 0)
    def _():
        win_ref[...] = jnp.zeros(win_ref.shape, F32)
        u_ref[...] = jnp.zeros(u_ref.shape, F32)

    seq_start = (i - 1) % tiles_per_seq == 0
    halo = win_ref[tm:tm + CONV_HALO, :]
    win_ref[0:CONV_HALO, :] = jnp.where(seq_start, jnp.zeros_like(halo), halo)
    win_ref[CONV_HALO:CONV_HALO + tm, :] = u_ref[...]
    h = _ffn_tile(x_ref[...], g1_ref, wgu_ref, wd_ref, act_ref, chunks,
                  side=_conv_pieces(win_ref, sh_ref, y_ref, cw_ref))
    h_ref[...] = h
    c_ref[...] = _conv_finish(y_ref, cb_ref, lg_ref, lb_ref)
    _inproj_tile(h, gm_ref[...], win_w_ref, pos_ref, invf_ref, u_ref, q_ref, k_ref, v_ref,
                 q_scale=q_scale)


def _front(x, positions, g1, w_gu, w_down, gm, w_in, conv_w, conv_b, ln_g, ln_b,
           *, seq, qk_cols, tm=512):
    n_tok, d = x.shape
    d_ff = w_down.shape[0]
    kw, conv_ch = conv_w.shape
    in_cols = w_in.shape[1]
    v_cols = in_cols - 2 * conv_ch - 2 * qk_cols
    n = _tiles(n_tok, tm)
    half = DA_HEAD_DIM // 2
    rows = _tiles(tm, ROPE_GROUPS)
    inv_freq = ROPE_THETA ** (-jnp.arange(0, DA_HEAD_DIM, 2, dtype=F32) / DA_HEAD_DIM)
    invf = jnp.tile(inv_freq, ROPE_GROUPS).reshape(1, LANES)
    pos = positions.astype(F32).reshape(n, ROPE_GROUPS, rows).transpose(0, 2, 1)
    pos = jnp.repeat(pos, half, axis=-1).reshape(n * rows, LANES)
    cur = lambda w, t=tm: pl.BlockSpec((t, w), lambda i: (jnp.minimum(i, n - 1), 0))
    lag = lambda w: pl.BlockSpec((tm, w), lambda i: (jnp.maximum(i - 1, 0), 0))
    return pl.pallas_call(
        functools.partial(_front_body, chunks=_ffn_chunks(d_ff, 256),
                          q_scale=DA_HEAD_DIM ** -0.5 * LOG2E, tiles_per_seq=_tiles(seq, tm)),
        out_shape=(jax.ShapeDtypeStruct((n_tok, d), F32),
                   jax.ShapeDtypeStruct((n_tok, qk_cols), BF16),
                   jax.ShapeDtypeStruct((n_tok, qk_cols), BF16),
                   jax.ShapeDtypeStruct((n_tok, v_cols), BF16),
                   jax.ShapeDtypeStruct((n_tok, conv_ch), BF16)),
        grid=(n + 1,),
        in_specs=[cur(d), cur(LANES, rows), _resident((1, d)), _resident((d, 2 * d_ff)),
                  _resident((d_ff, d)), _resident((1, d)), _resident((d, in_cols)),
                  _resident((1, LANES)), _resident((kw, conv_ch)), _resident((1, conv_ch)),
                  _resident((1, conv_ch)), _resident((1, conv_ch))],
        out_specs=(cur(d), cur(qk_cols), cur(qk_cols), cur(v_cols), lag(conv_ch)),
        scratch_shapes=[pltpu.VMEM((tm, d_ff), BF16),
                        pltpu.VMEM((tm, conv_ch), F32),
                        pltpu.VMEM((CONV_HALO + tm, conv_ch), F32),
                        pltpu.VMEM((SUBLANES, CONV_HALO + tm, LANES), F32),
                        pltpu.VMEM((tm, conv_ch), F32)],
        compiler_params=_params("arbitrary"),
        name="front",
    )(x, pos, g1, w_gu, w_down, gm, w_in, invf, conv_w, conv_b, ln_g, ln_b)


def _attn_body(lq1_ref, lk1_ref, lq2_ref, lk2_ref, dag_ref, q_ref, k_ref, v_ref, o_ref,
               *, lambda_init, tq):
    seq = q_ref.shape[0]
    tk = tq
    nq = seq // tq
    lane = lax.broadcasted_iota(jnp.int32, (tq, LANES), 1)
    key_le_query = (lax.broadcasted_iota(jnp.int32, (tk, tq), 0)
                    <= lax.broadcasted_iota(jnp.int32, (tk, tq), 1))
    lam = (jnp.exp(jnp.sum(lq1_ref[...] * lk1_ref[...], axis=-1, keepdims=True))
           - jnp.exp(jnp.sum(lq2_ref[...] * lk2_ref[...], axis=-1, keepdims=True))
           + lambda_init)
    vt = v_ref[...].astype(F32).T.astype(BF16)

    def scores(i):
        q = q_ref[i * tq:(i + 1) * tq, :]
        zero = jnp.zeros_like(q)
        qs = (jnp.where(lane < DA_HEAD_DIM, q, zero), jnp.where(lane >= DA_HEAD_DIM, q, zero))
        keys = k_ref[0:(i + 1) * tk, :]
        return [_dot_nt(keys, qz) for qz in qs]

    def probs(st):
        n_full = st.shape[0] - tk
        diag = jnp.where(key_le_query, st[n_full:], NEG_INF)
        m = jnp.max(diag, axis=0, keepdims=True)
        if n_full:
            full = st[:n_full]
            m = jnp.maximum(m, jnp.max(full, axis=0, keepdims=True))
            parts = [jnp.exp2(full - m), jnp.exp2(diag - m)]
        else:
            parts = [jnp.exp2(diag - m)]
        return parts, sum(jnp.sum(p, axis=0, keepdims=True) for p in parts)

    ahead = 3
    queue = [scores(i) for i in range(min(ahead, nq))]
    for i in range(nq):
        st = queue.pop(0)
        if i + ahead < nq:
            queue.append(scores(i + ahead))
        p1, l1 = probs(st[0])
        p2, l2 = probs(st[1])
        rho = lam * l1 * (1.0 / l2)
        a = jnp.concatenate([(x - rho * y).astype(BF16) for x, y in zip(p1, p2)], axis=0)
        o = (_dot(vt[:, 0:(i + 1) * tk], a) * (1.0 / l1)).T
        o_ref[i * tq:(i + 1) * tq, :] = (_rms(o, dag_ref[...]) * (1.0 - lambda_init)).astype(BF16)


def _attn(q, k, v, lq1, lk1, lq2, lk2, da_g, *, batch, seq, lambda_init, tq=256):
    n_tok, cols = q.shape
    heads = cols // LANES
    _tiles(seq, tq)
    blk = pl.BlockSpec((seq, LANES), lambda b, h: (b, h))
    vec = lambda w: pl.BlockSpec((1, w), lambda b, h: (0, 0))
    return pl.pallas_call(
        functools.partial(_attn_body, lambda_init=lambda_init, tq=tq),
        out_shape=jax.ShapeDtypeStruct((n_tok, cols), BF16),
        grid=(batch, heads),
        in_specs=[vec(DA_HEAD_DIM)] * 4 + [vec(LANES), blk, blk, blk],
        out_specs=blk,
        compiler_params=_params("parallel", "parallel"),
        name="diff_attn",
    )(lq1, lk1, lq2, lk2, da_g, q, k, v)


def _memkv_body(m_ref, g_ref, w_ref, o_ref):
    n = _rms(m_ref[...], g_ref[...]).astype(BF16)
    o_ref[...] = _dot(n, w_ref[...]).astype(BF16)


def _memkv(mem, norm_g, w_xkv, *, tm=512):
    n_mem, d = mem.shape
    cols = w_xkv.shape[1]
    return pl.pallas_call(
        _memkv_body,
        out_shape=jax.ShapeDtypeStruct((n_mem, cols), BF16),
        grid=(_tiles(n_mem, tm),),
        in_specs=[pl.BlockSpec((tm, d), lambda i: (i, 0)), _resident((1, d)),
                  _resident((d, cols))],
        out_specs=pl.BlockSpec((tm, cols), lambda i: (i, 0)),
        compiler_params=_params("parallel"),
        name="memkv",
    )(mem, norm_g, w_xkv)


def _back_body(*refs, x_scale, chunks, final):
    (h_ref, c_ref, a_ref, wout_ref, xg_ref, wxq_ref, kv_ref, wxo_ref,
     g2_ref, wgu_ref, wd_ref) = refs[:11]
    fin_ref = refs[11] if final else None
    o_ref, oc_ref, act_ref = refs[-3:]
    d = h_ref.shape[1]
    conv_ch = c_ref.shape[1]
    hd = d // X_HEADS
    h2 = (h_ref[...] + _dot(c_ref[...], wout_ref[0:conv_ch, :])
          + _dot(a_ref[...], wout_ref[conv_ch:, :]))
    n = _rms(h2, xg_ref[...]).astype(BF16)
    q = (_dot(n, wxq_ref[...]) * x_scale).astype(BF16)
    for hh in range(X_HEADS):
        qh = q[:, hh * hd:(hh + 1) * hd]
        s = _dot_nt(qh, kv_ref[:, hh * hd:(hh + 1) * hd])
        p = jnp.exp2(s - jnp.max(s, axis=-1, keepdims=True))
        inv = 1.0 / jnp.sum(p, axis=-1, keepdims=True)
        o = _dot(p.astype(BF16), kv_ref[:, d + hh * hd:d + (hh + 1) * hd]) * inv
        oc_ref[:, hh * hd:(hh + 1) * hd] = o.astype(BF16)
    h3 = h2 + _dot(oc_ref[...], wxo_ref[...])
    y = _ffn_tile(h3, g2_ref, wgu_ref, wd_ref, act_ref, chunks)
    if final:
        y = _rms(y, fin_ref[...])
    o_ref[...] = y


def _back(h, conv_o, attn_o, w_out, xg, w_xq, kv, w_xo, g2, w_gu, w_down, final_g=None,
          *, seq, mem_len, tm=512):
    n_tok, d = h.shape
    ch = conv_o.shape[1]
    d_ff = w_down.shape[0]
    per_b = _tiles(seq, tm)
    x_scale = (d // X_HEADS) ** -0.5 * LOG2E
    final = final_g is not None
    row = lambda w: pl.BlockSpec((tm, w), lambda i: (i, 0))
    in_specs = [row(d), row(ch), row(attn_o.shape[1]), _resident(w_out.shape),
                _resident((1, d)), _resident(w_xq.shape),
                pl.BlockSpec((mem_len, 2 * d), lambda i: (i // per_b, 0)),
                _resident(w_xo.shape), _resident((1, d)), _resident((d, 2 * d_ff)),
                _resident((d_ff, d))]
    args = [h, conv_o, attn_o, w_out, xg, w_xq, kv, w_xo, g2, w_gu, w_down]
    if final:
        in_specs.append(_resident((1, d)))
        args.append(final_g)
    return pl.pallas_call(
        functools.partial(_back_body, x_scale=x_scale, chunks=_ffn_chunks(d_ff), final=final),
        out_shape=jax.ShapeDtypeStruct((n_tok, d), F32),
        grid=(_tiles(n_tok, tm),),
        in_specs=in_specs,
        out_specs=row(d),
        scratch_shapes=[pltpu.VMEM((tm, d), BF16), pltpu.VMEM((tm, d_ff), BF16)],
        compiler_params=_params("parallel"),
        name="back",
    )(*args)


def kernel(x, mem, positions, ffn1_norm, ffn1_w_gu, ffn1_w_down, mix_norm, w_in, conv_w, conv_b,
           conv_ln_g, conv_ln_b, lambda_q1, lambda_k1, lambda_q2, lambda_k2, da_norm_g, w_out,
           xattn_norm, mem_norm, w_xq, w_xkv, w_xo, ffn2_norm, ffn2_w_gu, ffn2_w_down,
           final_norm):
    batch, seq, d = x.shape
    mem_len = mem.shape[1]
    depth = ffn1_norm.shape[0]
    qk_cols = DA_HEADS * 2 * DA_HEAD_DIM
    n_tok = batch * seq

    vec = lambda a: a.reshape(1, -1).astype(F32)
    bf = lambda a: a.astype(BF16)

    pos = positions.reshape(n_tok)
    h = x.reshape(n_tok, d)
    mem2 = mem.reshape(batch * mem_len, d)
    for l in range(depth):
        lambda_init = 0.8 - 0.6 * math.exp(-0.3 * l)
        h, q, k, v, conv_o = _front(h, pos, vec(ffn1_norm[l]), bf(ffn1_w_gu[l]),
                                    bf(ffn1_w_down[l]), vec(mix_norm[l]), bf(w_in[l]),
                                    conv_w[l], vec(conv_b[l]), vec(conv_ln_g[l]),
                                    vec(conv_ln_b[l]), seq=seq, qk_cols=qk_cols)
        attn_o = _attn(q, k, v, vec(lambda_q1[l]), vec(lambda_k1[l]), vec(lambda_q2[l]),
                       vec(lambda_k2[l]), vec(da_norm_g[l]), batch=batch, seq=seq,
                       lambda_init=lambda_init)
        kv = _memkv(mem2, vec(mem_norm[l]), bf(w_xkv[l]))
        last = l == depth - 1
        h = _back(h, conv_o, attn_o, bf(w_out[l]), vec(xattn_norm[l]), bf(w_xq[l]), kv,
                  bf(w_xo[l]), vec(ffn2_norm[l]), bf(ffn2_w_gu[l]), bf(ffn2_w_down[l]),
                  vec(final_norm) if last else None, seq=seq, mem_len=mem_len)
    if depth == 0:
        raise ValueError("depth must be >= 1")
    return h.reshape(batch, seq, d)
```

```python
import functools
import math

import jax
import jax.numpy as jnp
from jax import lax
from jax.experimental import pallas as pl
from jax.experimental.pallas import tpu as pltpu

F32 = jnp.float32
BF16 = jnp.bfloat16

NORM_EPS = 1e-6
NEG_INF = -1e30
ROPE_THETA = 10000.0
LOG2E = math.log2(math.e)

DA_HEADS = 4
DA_HEAD_DIM = 64
X_HEADS = 4
LANES = 128
SUBLANES = 8
ROPE_GROUPS = LANES // (DA_HEAD_DIM // 2)
CONV_HALO = 32
VMEM_LIMIT = 56 * 1024 * 1024


def _rms(x, g):
    ms = jnp.mean(x * x, axis=-1, keepdims=True)
    return x * lax.rsqrt(ms + NORM_EPS) * g


def _sigmoid(x):
    return 1.0 / (1.0 + jnp.exp(-x))


def _dot(a, b):
    return jnp.dot(a, b, preferred_element_type=F32)


def _dot_nt(a, b):
    return lax.dot_general(a, b, (((1,), (1,)), ((), ())), preferred_element_type=F32)


def _resident(shape):
    return pl.BlockSpec(shape, lambda *_: (0,) * len(shape), pipeline_mode=pl.Buffered(1))


def _params(*sem):
    return pltpu.CompilerParams(dimension_semantics=sem, vmem_limit_bytes=VMEM_LIMIT)


def _tiles(n, t):
    if n % t:
        raise ValueError(f"extent {n} is not a multiple of tile {t}")
    return n // t


def _ffn_tile(x, g_ref, wgu_ref, wd_ref, act_ref, chunks, side=()):
    side = list(side)
    d_ff, d = wd_ref.shape
    if not side:
        n = _rms(x, g_ref[...]).astype(BF16)
        for c0, c1 in chunks:
            g = _dot(n, wgu_ref[:, c0:c1])
            u = _dot(n, wgu_ref[:, d_ff + c0:d_ff + c1])
            act_ref[:, c0:c1] = (g * _sigmoid(g) * u).astype(BF16)
        return x + 0.5 * _dot(act_ref[...], wd_ref[...])

    step = chunks[0][1] - chunks[0][0]
    pieces_left = [len(chunks) + d // step]

    def tied(val):
        take = len(side) // pieces_left[0]
        pieces_left[0] -= 1
        for _ in range(take):
            zero = side.pop(0)()
            val = val + jnp.concatenate([zero] * (val.shape[1] // LANES), axis=1)
        return val

    n = _rms(x, g_ref[...]).astype(BF16)
    for c0, c1 in chunks:
        g = _dot(n, wgu_ref[:, c0:c1])
        u = _dot(n, wgu_ref[:, d_ff + c0:d_ff + c1])
        act_ref[:, c0:c1] = tied(g * _sigmoid(g) * u).astype(BF16)
    outs = [tied(x[:, c:c + step] + 0.5 * _dot(act_ref[...], wd_ref[:, c:c + step]))
            for c in range(0, d, step)]
    return jnp.concatenate(outs, axis=1)


def _ffn_chunks(d_ff, step=1024):
    return tuple((c, min(c + step, d_ff)) for c in range(0, d_ff, step))


def _inproj_tile(h, g, w_ref, pos_ref, invf_ref, u_ref, q_ref, k_ref, v_ref, *, q_scale):
    tm = h.shape[0]
    conv_ch = u_ref.shape[1]
    qk_cols = q_ref.shape[1]
    n = _rms(h, g).astype(BF16)
    ag = _dot(n, w_ref[:, 0:2 * conv_ch])
    u_ref[...] = ag[:, :conv_ch] * _sigmoid(ag[:, conv_ch:])
    half = DA_HEAD_DIM // 2
    rows = tm // ROPE_GROUPS
    ang = pos_ref[...] * invf_ref[...]
    cos_c = jnp.cos(ang)
    sin_c = jnp.sin(ang)
    lane = lax.broadcasted_iota(jnp.int32, (rows, LANES), 1)
    first_half = (lane % DA_HEAD_DIM) < half

    def spread(x, r):
        x = jnp.where(lane // half == r, x, 0.0)
        return sum((pltpu.roll(x, s * half, 1) for s in range(1, ROPE_GROUPS)), x)

    q0 = 2 * conv_ch
    k0 = q0 + qk_cols
    v0 = k0 + qk_cols
    q = _dot(n, w_ref[:, q0:k0])
    k = _dot(n, w_ref[:, k0:v0])
    for r in range(ROPE_GROUPS):
        cos = spread(cos_c, r)
        sin = spread(sin_c, r)
        sin = jnp.where(first_half, -sin, sin)

        def rope(x):
            partner = jnp.where(first_half, pltpu.roll(x, LANES - half, 1),
                                pltpu.roll(x, half, 1))
            return x * cos + partner * sin

        rs = slice(r * rows, (r + 1) * rows)
        for c in range(0, qk_cols, LANES):
            q_ref[rs, c:c + LANES] = (rope(q[rs, c:c + LANES]) * q_scale).astype(BF16)
            k_ref[rs, c:c + LANES] = rope(k[rs, c:c + LANES]).astype(BF16)
    v_ref[...] = _dot(n, w_ref[:, v0:]).astype(BF16)


def _zero_row_after(x):
    bits = pltpu.bitcast(x, jnp.uint32)
    while bits.shape[0] > SUBLANES:
        half = bits.shape[0] // 2
        bits = bits[:half] | bits[half:]
    return pltpu.bitcast((bits >> 16) >> 16, F32)[0:1, :]


def _conv_pieces(win_ref, sh_ref, y_ref, w_ref, *, block=128):
    ts, ch = y_ref.shape
    kw = w_ref.shape[0]
    base = CONV_HALO - (kw - 1)
    chunk = 32

    def piece(c, b0):
        if b0 == 0:
            for r in range(1, SUBLANES):
                rows = CONV_HALO + ts - SUBLANES
                sh_ref[r, 0:rows, :] = win_ref[r:r + rows, c:c + LANES]
        zero = None
        for t0 in range(b0, b0 + block, chunk):
            acc = jnp.zeros((chunk, LANES), F32)
            for j in range(kw):
                r = (base + j) % SUBLANES
                a = (base + j) // SUBLANES * SUBLANES + t0
                x = win_ref[a:a + chunk, c:c + LANES] if r == 0 else sh_ref[r, a:a + chunk, :]
                acc = acc + w_ref[j:j + 1, c:c + LANES] * x
            y_ref[t0:t0 + chunk, c:c + LANES] = acc
            z = _zero_row_after(acc)
            zero = z if zero is None else zero + z
        return zero

    return [functools.partial(piece, c, b0)
            for c in range(0, ch, LANES) for b0 in range(0, ts, block)]


def _conv_finish(y_ref, b_ref, lg_ref, lb_ref):
    y = y_ref[...] + b_ref[...]
    mu = jnp.mean(y, axis=-1, keepdims=True)
    yc = y - mu
    var = jnp.mean(yc * yc, axis=-1, keepdims=True)
    z = yc * lax.rsqrt(var + NORM_EPS) * lg_ref[...] + lb_ref[...]
    return (z * _sigmoid(z)).astype(BF16)


def _front_body(x_ref, pos_ref, g1_ref, wgu_ref, wd_ref, gm_ref, win_w_ref, invf_ref,
                cw_ref, cb_ref, lg_ref, lb_ref,
                h_ref, q_ref, k_ref, v_ref, c_ref,
                act_ref, u_ref, win_ref, sh_ref, y_ref, *, chunks, q_scale, tiles_per_seq):
    i = pl.program_id(0)
    tm = x_ref.shape[0]

    @pl.when(i == 0)
    def _():
        win_ref[...] = jnp.zeros(win_ref.shape, F32)
        u_ref[...] = jnp.zeros(u_ref.shape, F32)

    seq_start = (i - 1) % tiles_per_seq == 0
    halo = win_ref[tm:tm + CONV_HALO, :]
    win_ref[0:CONV_HALO, :] = jnp.where(seq_start, jnp.zeros_like(halo), halo)
    win_ref[CONV_HALO:CONV_HALO + tm, :] = u_ref[...]
    h = _ffn_tile(x_ref[...], g1_ref, wgu_ref, wd_ref, act_ref, chunks,
                  side=_conv_pieces(win_ref, sh_ref, y_ref, cw_ref))
    h_ref[...] = h
    c_ref[...] = _conv_finish(y_ref, cb_ref, lg_ref, lb_ref)
    _inproj_tile(h, gm_ref[...], win_w_ref, pos_ref, invf_ref, u_ref, q_ref, k_ref, v_ref,
                 q_scale=q_scale)


def _front(x, positions, g1, w_gu, w_down, gm, w_in, conv_w, conv_b, ln_g, ln_b,
           *, seq, qk_cols, tm=512):
    n_tok, d = x.shape
    d_ff = w_down.shape[0]
    kw, conv_ch = conv_w.shape
    in_cols = w_in.shape[1]
    v_cols = in_cols - 2 * conv_ch - 2 * qk_cols
    n = _tiles(n_tok, tm)
    half = DA_HEAD_DIM // 2
    rows = _tiles(tm, ROPE_GROUPS)
    inv_freq = ROPE_THETA ** (-jnp.arange(0, DA_HEAD_DIM, 2, dtype=F32) / DA_HEAD_DIM)
    invf = jnp.tile(inv_freq, ROPE_GROUPS).reshape(1, LANES)
    pos = positions.astype(F32).reshape(n, ROPE_GROUPS, rows).transpose(0, 2, 1)
    pos = jnp.repeat(pos, half, axis=-1).reshape(n * rows, LANES)
    cur = lambda w, t=tm: pl.BlockSpec((t, w), lambda i: (jnp.minimum(i, n - 1), 0))
    lag = lambda w: pl.BlockSpec((tm, w), lambda i: (jnp.maximum(i - 1, 0), 0))
    return pl.pallas_call(
        functools.partial(_front_body, chunks=_ffn_chunks(d_ff, 256),
                          q_scale=DA_HEAD_DIM ** -0.5 * LOG2E, tiles_per_seq=_tiles(seq, tm)),
        out_shape=(jax.ShapeDtypeStruct((n_tok, d), F32),
                   jax.ShapeDtypeStruct((n_tok, qk_cols), BF16),
                   jax.ShapeDtypeStruct((n_tok, qk_cols), BF16),
                   jax.ShapeDtypeStruct((n_tok, v_cols), BF16),
                   jax.ShapeDtypeStruct((n_tok, conv_ch), BF16)),
        grid=(n + 1,),
        in_specs=[cur(d), cur(LANES, rows), _resident((1, d)), _resident((d, 2 * d_ff)),
                  _resident((d_ff, d)), _resident((1, d)), _resident((d, in_cols)),
                  _resident((1, LANES)), _resident((kw, conv_ch)), _resident((1, conv_ch)),
                  _resident((1, conv_ch)), _resident((1, conv_ch))],
        out_specs=(cur(d), cur(qk_cols), cur(qk_cols), cur(v_cols), lag(conv_ch)),
        scratch_shapes=[pltpu.VMEM((tm, d_ff), BF16),
                        pltpu.VMEM((tm, conv_ch), F32),
                        pltpu.VMEM((CONV_HALO + tm, conv_ch), F32),
                        pltpu.VMEM((SUBLANES, CONV_HALO + tm, LANES), F32),
                        pltpu.VMEM((tm, conv_ch), F32)],
        compiler_params=_params("arbitrary"),
        name="front",
    )(x, pos, g1, w_gu, w_down, gm, w_in, invf, conv_w, conv_b, ln_g, ln_b)


def _attn_body(lq1_ref, lk1_ref, lq2_ref, lk2_ref, dag_ref, q_ref, k_ref, v_ref, o_ref,
               *, lambda_init, tq):
    seq = q_ref.shape[0]
    tk = tq
    nq = seq // tq
    lane = lax.broadcasted_iota(jnp.int32, (tq, LANES), 1)
    key_le_query = (lax.broadcasted_iota(jnp.int32, (tk, tq), 0)
                    <= lax.broadcasted_iota(jnp.int32, (tk, tq), 1))
    lam = (jnp.exp(jnp.sum(lq1_ref[...] * lk1_ref[...], axis=-1, keepdims=True))
           - jnp.exp(jnp.sum(lq2_ref[...] * lk2_ref[...], axis=-1, keepdims=True))
           + lambda_init)
    hps = q_ref.shape[1] // LANES
    cols = [slice(s * LANES, (s + 1) * LANES) for s in range(hps)]
    vts = [v_ref[:, c].astype(F32).T.astype(BF16) for c in cols]

    def scores(job):
        i, s = job
        q = q_ref[i * tq:(i + 1) * tq, cols[s]]
        zero = jnp.zeros_like(q)
        qs = (jnp.where(lane < DA_HEAD_DIM, q, zero), jnp.where(lane >= DA_HEAD_DIM, q, zero))
        keys = k_ref[0:(i + 1) * tk, cols[s]]
        return [_dot_nt(keys, qz) for qz in qs]

    def probs(st):
        n_full = st.shape[0] - tk
        diag = jnp.where(key_le_query, st[n_full:], NEG_INF)
        m = jnp.max(diag, axis=0, keepdims=True)
        if n_full:
            full = st[:n_full]
            m = jnp.maximum(m, jnp.max(full, axis=0, keepdims=True))
            parts = [jnp.exp2(full - m), jnp.exp2(diag - m)]
        else:
            parts = [jnp.exp2(diag - m)]
        return parts, sum(jnp.sum(p, axis=0, keepdims=True) for p in parts)

    ahead = 5
    jobs = [(i, s) for i in range(nq) for s in range(hps)]
    queue = [scores(job) for job in jobs[:ahead]]
    for n, (i, s) in enumerate(jobs):
        st = queue.pop(0)
        if n + ahead < len(jobs):
            queue.append(scores(jobs[n + ahead]))
        p1, l1 = probs(st[0])
        p2, l2 = probs(st[1])
        rho = lam * l1 * (1.0 / l2)
        a = jnp.concatenate([(x - rho * y).astype(BF16) for x, y in zip(p1, p2)], axis=0)
        o = (_dot(vts[s][:, 0:(i + 1) * tk], a) * (1.0 / l1)).T
        o_ref[i * tq:(i + 1) * tq, cols[s]] = (_rms(o, dag_ref[...])
                                               * (1.0 - lambda_init)).astype(BF16)


def _attn(q, k, v, lq1, lk1, lq2, lk2, da_g, *, batch, seq, lambda_init, tq=256, hps=2):
    n_tok, cols = q.shape
    heads = _tiles(cols // LANES, hps)
    _tiles(seq, tq)
    blk = pl.BlockSpec((seq, hps * LANES), lambda b, h: (b, h))
    vec = lambda w: pl.BlockSpec((1, w), lambda b, h: (0, 0))
    return pl.pallas_call(
        functools.partial(_attn_body, lambda_init=lambda_init, tq=tq),
        out_shape=jax.ShapeDtypeStruct((n_tok, cols), BF16),
        grid=(batch, heads),
        in_specs=[vec(DA_HEAD_DIM)] * 4 + [vec(LANES), blk, blk, blk],
        out_specs=blk,
        compiler_params=_params("parallel", "parallel"),
        name="diff_attn",
    )(lq1, lk1, lq2, lk2, da_g, q, k, v)


def _memkv_body(m_ref, g_ref, w_ref, o_ref):
    n = _rms(m_ref[...], g_ref[...]).astype(BF16)
    o_ref[...] = _dot(n, w_ref[...]).astype(BF16)


def _memkv(mem, norm_g, w_xkv, *, tm=512):
    n_mem, d = mem.shape
    cols = w_xkv.shape[1]
    return pl.pallas_call(
        _memkv_body,
        out_shape=jax.ShapeDtypeStruct((n_mem, cols), BF16),
        grid=(_tiles(n_mem, tm),),
        in_specs=[pl.BlockSpec((tm, d), lambda i: (i, 0)), _resident((1, d)),
                  _resident((d, cols))],
        out_specs=pl.BlockSpec((tm, cols), lambda i: (i, 0)),
        compiler_params=_params("parallel"),
        name="memkv",
    )(mem, norm_g, w_xkv)


def _back_body(*refs, x_scale, chunks, final):
    (h_ref, c_ref, a_ref, wout_ref, xg_ref, wxq_ref, kv_ref, wxo_ref,
     g2_ref, wgu_ref, wd_ref) = refs[:11]
    fin_ref = refs[11] if final else None
    o_ref, oc_ref, act_ref = refs[-3:]
    d = h_ref.shape[1]
    conv_ch = c_ref.shape[1]
    hd = d // X_HEADS
    h2 = (h_ref[...] + _dot(c_ref[...], wout_ref[0:conv_ch, :])
          + _dot(a_ref[...], wout_ref[conv_ch:, :]))
    n = _rms(h2, xg_ref[...]).astype(BF16)
    q = (_dot(n, wxq_ref[...]) * x_scale).astype(BF16)
    for hh in range(X_HEADS):
        qh = q[:, hh * hd:(hh + 1) * hd]
        s = _dot_nt(qh, kv_ref[:, hh * hd:(hh + 1) * hd])
        p = jnp.exp2(s - jnp.max(s, axis=-1, keepdims=True))
        inv = 1.0 / jnp.sum(p, axis=-1, keepdims=True)
        o = _dot(p.astype(BF16), kv_ref[:, d + hh * hd:d + (hh + 1) * hd]) * inv
        oc_ref[:, hh * hd:(hh + 1) * hd] = o.astype(BF16)
    h3 = h2 + _dot(oc_ref[...], wxo_ref[...])
    y = _ffn_tile(h3, g2_ref, wgu_ref, wd_ref, act_ref, chunks)
    if final:
        y = _rms(y, fin_ref[...])
    o_ref[...] = y


def _back(h, conv_o, attn_o, w_out, xg, w_xq, kv, w_xo, g2, w_gu, w_down, final_g=None,
          *, seq, mem_len, tm=512):
    n_tok, d = h.shape
    ch = conv_o.shape[1]
    d_ff = w_down.shape[0]
    per_b = _tiles(seq, tm)
    x_scale = (d // X_HEADS) ** -0.5 * LOG2E
    final = final_g is not None
    row = lambda w: pl.BlockSpec((tm, w), lambda i: (i, 0))
    in_specs = [row(d), row(ch), row(attn_o.shape[1]), _resident(w_out.shape),
                _resident((1, d)), _resident(w_xq.shape),
                pl.BlockSpec((mem_len, 2 * d), lambda i: (i // per_b, 0)),
                _resident(w_xo.shape), _resident((1, d)), _resident((d, 2 * d_ff)),
                _resident((d_ff, d))]
    args = [h, conv_o, attn_o, w_out, xg, w_xq, kv, w_xo, g2, w_gu, w_down]
    if final:
        in_specs.append(_resident((1, d)))
        args.append(final_g)
    return pl.pallas_call(
        functools.partial(_back_body, x_scale=x_scale, chunks=_ffn_chunks(d_ff), final=final),
        out_shape=jax.ShapeDtypeStruct((n_tok, d), F32),
        grid=(_tiles(n_tok, tm),),
        in_specs=in_specs,
        out_specs=row(d),
        scratch_shapes=[pltpu.VMEM((tm, d), BF16), pltpu.VMEM((tm, d_ff), BF16)],
        compiler_params=_params("parallel"),
        name="back",
    )(*args)


def kernel(x, mem, positions, ffn1_norm, ffn1_w_gu, ffn1_w_down, mix_norm, w_in, conv_w, conv_b,
           conv_ln_g, conv_ln_b, lambda_q1, lambda_k1, lambda_q2, lambda_k2, da_norm_g, w_out,
           xattn_norm, mem_norm, w_xq, w_xkv, w_xo, ffn2_norm, ffn2_w_gu, ffn2_w_down,
           final_norm):
    batch, seq, d = x.shape
    mem_len = mem.shape[1]
    depth = ffn1_norm.shape[0]
    qk_cols = DA_HEADS * 2 * DA_HEAD_DIM
    n_tok = batch * seq

    vec = lambda a: a.reshape(1, -1).astype(F32)
    bf = lambda a: a.astype(BF16)

    pos = positions.reshape(n_tok)
    h = x.reshape(n_tok, d)
    mem2 = mem.reshape(batch * mem_len, d)
    for l in range(depth):
        lambda_init = 0.8 - 0.6 * math.exp(-0.3 * l)
        h, q, k, v, conv_o = _front(h, pos, vec(ffn1_norm[l]), bf(ffn1_w_gu[l]),
                                    bf(ffn1_w_down[l]), vec(mix_norm[l]), bf(w_in[l]),
                                    conv_w[l], vec(conv_b[l]), vec(conv_ln_g[l]),
                                    vec(conv_ln_b[l]), seq=seq, qk_cols=qk_cols)
        attn_o = _attn(q, k, v, vec(lambda_q1[l]), vec(lambda_k1[l]), vec(lambda_q2[l]),
                       vec(lambda_k2[l]), vec(da_norm_g[l]), batch=batch, seq=seq,
                       lambda_init=lambda_init)
        kv = _memkv(mem2, vec(mem_norm[l]), bf(w_xkv[l]))
        last = l == depth - 1
        h = _back(h, conv_o, attn_o, bf(w_out[l]), vec(xattn_norm[l]), bf(w_xq[l]), kv,
                  bf(w_xo[l]), vec(ffn2_norm[l]), bf(ffn2_w_gu[l]), bf(ffn2_w_down[l]),
                  vec(final_norm) if last else None, seq=seq, mem_len=mem_len)
    if depth == 0:
        raise ValueError("depth must be >= 1")
    return h.reshape(batch, seq, d)
```
